```python
import math
import jax, jax.numpy as jnp
from jax import lax
import numpy as np

D_MODEL = 1024
BATCH = 2
SEQ = 8192
DEPTH = 2
DEC_BATCH = 32
DEC_SEQ = 8
PAST_LEN = 8192
PAGE_SIZE = 128

N_HEADS = 8
HEAD_DIM = 64
D_ATTN = N_HEADS * HEAD_DIM
ROPE_DIM = HEAD_DIM // 4
ROPE_THETA = 500000.0
MOBA_BLOCK = 256
MOBA_TOPK = 3
Q_CHUNK = 64
D_CONV = 512
CONV_WIDTH = 31
D_POOL = 512
POOL_WINDOWS = (2, 4, 8, 16)
N_POOL_GROUPS = len(POOL_WINDOWS)
POOL_GROUP = D_POOL // N_POOL_GROUPS
POOL_STATE = max(POOL_WINDOWS) - 1
N_BRANCHES = 3
SPLIT_IDX = [D_ATTN, 2 * D_ATTN, 3 * D_ATTN, 3 * D_ATTN + 2 * D_CONV, 3 * D_ATTN + 2 * D_CONV + D_POOL]
D_IN = 3 * D_ATTN + 2 * D_CONV + D_POOL + N_BRANCHES * D_MODEL
D_FF = 2816
N_EXPERTS = 8
TOP_K_EXPERTS = 2
D_FF_EXPERT = 2816
N_DENSE = (DEPTH + 1) // 2
N_MOE = DEPTH // 2
EPS = 1e-6
F32 = jnp.float32

kernel_name = "hybrid_conv_moba_pool_decoder_step"


def rms_norm(x, g):
    x32 = x.astype(F32)
    y = x32 * lax.rsqrt(jnp.mean(x32 * x32, axis=-1, keepdims=True) + EPS)
    return (y * g.astype(F32)).astype(x.dtype)


def layer_norm(x, g, b):
    x32 = x.astype(F32)
    mu = jnp.mean(x32, axis=-1, keepdims=True)
    xc = x32 - mu
    var = jnp.mean(xc * xc, axis=-1, keepdims=True)
    return xc * lax.rsqrt(var + EPS) * g.astype(F32) + b.astype(F32)


def partial_rope(x, pos):
    half = ROPE_DIM // 2
    inv_freq = jnp.exp(jnp.arange(half, dtype=F32) * (-2.0 * math.log(ROPE_THETA) / ROPE_DIM))
    ang = pos.astype(F32)[:, None] * inv_freq[None, :]
    cos = jnp.cos(ang)[None, :, None, :]
    sin = jnp.sin(ang)[None, :, None, :]
    xr = x[..., :ROPE_DIM].astype(F32)
    x1, x2 = xr[..., :half], xr[..., half:]
    rot = jnp.concatenate([x1 * cos - x2 * sin, x2 * cos + x1 * sin], axis=-1).astype(x.dtype)
    return jnp.concatenate([rot, x[..., ROPE_DIM:]], axis=-1)


def moba_attention(q, q_pos, k_all, v_all, q_chunk):
    B, Lq = q.shape[0], q.shape[1]
    Lk = k_all.shape[1]
    nb = -(-Lk // MOBA_BLOCK)
    pad = nb * MOBA_BLOCK - Lk

    def to_blocks(t):
        if pad:
            t = jnp.pad(t, ((0, 0), (0, pad), (0, 0), (0, 0)))
        return t.reshape(B, nb, MOBA_BLOCK, N_HEADS, HEAD_DIM)

    kb, vb = to_blocks(k_all), to_blocks(v_all)
    k_mean = jnp.mean(kb.astype(F32), axis=2)
    n_sel = min(MOBA_TOPK, nb)
    bi = jnp.arange(B)[:, None, None, None]
    hi = jnp.arange(N_HEADS)[None, None, :, None]
    offs = jnp.arange(MOBA_BLOCK)
    scale = HEAD_DIM ** -0.5

    def chunk(args):
        qc, pc = args
        own = pc // MOBA_BLOCK
        gate = jnp.einsum('bchd,bnhd->bchn', qc.astype(F32), k_mean)
        is_past = jnp.arange(nb)[None, :] < own[:, None]
        gate = jnp.where(is_past[None, :, None, :], gate, -jnp.inf)
        _, sel = lax.top_k(gate, n_sel)
        sel_ok = sel < own[None, :, None, None]
        own_b = jnp.broadcast_to(own[None, :, None, None], sel.shape[:3] + (1,)).astype(sel.dtype)
        blk = jnp.concatenate([sel, own_b], axis=-1)
        ok = jnp.concatenate([sel_ok, jnp.ones(own_b.shape, dtype=bool)], axis=-1)
        kg = kb[bi, blk, :, hi]
        vg = vb[bi, blk, :, hi]
        s = jnp.einsum('bchd,bchskd->bchsk', qc, kg, preferred_element_type=F32) * scale
        key_pos = blk[..., None] * MOBA_BLOCK + offs
        mask = ok[..., None] & (key_pos <= pc[None, :, None, None, None])
        s = jnp.where(mask, s, -jnp.inf)
        p = jax.nn.softmax(s.reshape(s.shape[:3] + (-1,)), axis=-1).reshape(s.shape)
        return jnp.einsum('bchsk,bchskd->bchd', p.astype(vg.dtype), vg)

    n = Lq // q_chunk
    qs = q.reshape(B, n, q_chunk, N_HEADS, HEAD_DIM).transpose(1, 0, 2, 3, 4)
    ps = q_pos.reshape(n, q_chunk)
    out = lax.map(chunk, (qs, ps))
    return out.transpose(1, 0, 2, 3, 4).reshape(B, Lq, D_ATTN)


def conformer_conv(u, prev, conv_w, conv_b, ln_g, ln_b, w_conv_out):
    ext = jnp.concatenate([prev.astype(u.dtype), u], axis=1)
    y = lax.conv_general_dilated(ext, conv_w[:, None, :].astype(ext.dtype), window_strides=(1,),
                                 padding='VALID', dimension_numbers=('NWC', 'WIO', 'NWC'),
                                 feature_group_count=D_CONV) + conv_b
    y = jax.nn.silu(layer_norm(y, ln_g, ln_b)).astype(u.dtype)
    return y @ w_conv_out, ext[:, -(CONV_WIDTH - 1):]


def multiscale_pool(z, prev, n_prev, pool_lin, pool_scale, w_pool_out):
    B, L = z.shape[0], z.shape[1]
    ext_raw = jnp.concatenate([prev.astype(z.dtype), z], axis=1)
    ext = ext_raw.astype(F32)
    csum = jnp.concatenate([jnp.zeros((B, 1, D_POOL), F32), jnp.cumsum(ext, axis=1)], axis=1)
    t = jnp.arange(L)
    z32 = z.astype(F32)
    outs = []
    for g, w in enumerate(POOL_WINDOWS):
        c = slice(g * POOL_GROUP, (g + 1) * POOL_GROUP)
        end = csum[:, POOL_STATE + 1:POOL_STATE + 1 + L, c]
        start = csum[:, POOL_STATE + 1 - w:POOL_STATE + 1 - w + L, c]
        cnt = jnp.minimum(w, t + 1 + n_prev).astype(F32)[None, :, None]
        outs.append((end - start) / cnt - z32[..., c])
    d = jnp.stack(outs, axis=2)
    m = jnp.einsum('blgc,gcd->blgd', d, pool_lin.astype(F32)).reshape(B, L, D_POOL)
    m = (m * pool_scale.astype(F32)).astype(z.dtype)
    return m @ w_pool_out, ext_raw[:, -POOL_STATE:]


def token_mixers(h, pos, k_past, v_past, conv_prev, pool_prev, n_pool_prev, q_chunk,
                 w_in, w_attn_out, conv_w, conv_b, conv_ln_g, conv_ln_b, w_conv_out,
                 pool_lin, pool_scale, w_pool_out, w_out):
    B, L = h.shape[0], h.shape[1]
    proj = h @ w_in
    q, k, v, glu, pz, gz = jnp.split(proj, SPLIT_IDX, axis=-1)
    q = partial_rope(q.reshape(B, L, N_HEADS, HEAD_DIM), pos)
    k = partial_rope(k.reshape(B, L, N_HEADS, HEAD_DIM), pos)
    v = v.reshape(B, L, N_HEADS, HEAD_DIM)
    k_all = jnp.concatenate([k_past.astype(k.dtype), k], axis=1)
    v_all = jnp.concatenate([v_past.astype(v.dtype), v], axis=1)
    a = moba_attention(q, pos, k_all, v_all, q_chunk) @ w_attn_out
    u = glu[..., :D_CONV] * jax.nn.sigmoid(glu[..., D_CONV:])
    c, conv_state = conformer_conv(u, conv_prev, conv_w, conv_b, conv_ln_g, conv_ln_b, w_conv_out)
    p, pool_state = multiscale_pool(pz, pool_prev, n_pool_prev, pool_lin, pool_scale, w_pool_out)
    g = jax.nn.sigmoid(gz.astype(F32)).reshape(B, L, N_BRANCHES, D_MODEL)
    merged = (g[:, :, 0] * a.astype(F32) + g[:, :, 1] * c.astype(F32) + g[:, :, 2] * p.astype(F32)).astype(h.dtype)
    return merged @ w_out, k, v, conv_state, pool_state


def swiglu(x, w1, w3, w2):
    return (jax.nn.silu(x @ w1) * (x @ w3)) @ w2


def moe_swiglu(x, router, w1, w3, w2):
    logits = (x @ router).astype(F32)
    top_v, top_i = lax.top_k(logits, TOP_K_EXPERTS)
    top_w = jax.nn.softmax(top_v, axis=-1)
    comb = jnp.sum(jax.nn.one_hot(top_i, N_EXPERTS, dtype=F32) * top_w[..., None], axis=-2)
    y = jnp.zeros(x.shape, F32)
    for e in range(N_EXPERTS):
        y = y + comb[..., e:e + 1] * swiglu(x, w1[e], w3[e], w2[e]).astype(F32)
    return y.astype(x.dtype)


def forward(x, pos, cache_k, cache_v, page_table, state_conv, state_pool,
            norm_mix, norm_ffn, norm_final, w_in, w_attn_out, conv_w, conv_b, conv_ln_g, conv_ln_b,
            w_conv_out, pool_lin, pool_scale, w_pool_out, w_out,
            ffn_w1, ffn_w3, ffn_w2, moe_router, moe_w1, moe_w3, moe_w2):
    B, L = x.shape[0], x.shape[1]
    q_chunk = Q_CHUNK if L % Q_CHUNK == 0 else L
    n_pool_prev = 0 if state_pool is None else POOL_STATE
    ks, vs, cs, ps = [], [], [], []
    for l in range(DEPTH):
        if cache_k is None:
            k_past = jnp.zeros((B, 0, N_HEADS, HEAD_DIM), x.dtype)
            v_past = k_past
            conv_prev = jnp.zeros((B, CONV_WIDTH - 1, D_CONV), x.dtype)
            pool_prev = jnp.zeros((B, POOL_STATE, D_POOL), x.dtype)
        else:
            k_past = cache_k[l][page_table].reshape(B, -1, N_HEADS, HEAD_DIM)
            v_past = cache_v[l][page_table].reshape(B, -1, N_HEADS, HEAD_DIM)
            conv_prev = state_conv[l]
            pool_prev = state_pool[l]
        h = rms_norm(x, norm_mix[l])
        mix, k_new, v_new, c_st, p_st = token_mixers(
            h, pos, k_past, v_past, conv_prev, pool_prev, n_pool_prev, q_chunk,
            w_in[l], w_attn_out[l], conv_w[l], conv_b[l], conv_ln_g[l], conv_ln_b[l], w_conv_out[l],
            pool_lin[l], pool_scale[l], w_pool_out[l], w_out[l])
        x = x + mix
        h = rms_norm(x, norm_ffn[l])
        if l % 2 == 0:
            i = l // 2
            x = x + swiglu(h, ffn_w1[i], ffn_w3[i], ffn_w2[i])
        else:
            i = l // 2
            x = x + moe_swiglu(h, moe_router[i], moe_w1[i], moe_w3[i], moe_w2[i])
        ks.append(k_new); vs.append(v_new); cs.append(c_st); ps.append(p_st)
    y = rms_norm(x, norm_final)
    return y, jnp.stack(ks), jnp.stack(vs), jnp.stack(cs), jnp.stack(ps)


def setup_inputs(seed: int = 0) -> dict:
    key = jax.random.key(seed)
    kk = jax.random.split(key, 32)
    n_pages = PAST_LEN // PAGE_SIZE
    n_pool_pages = (DEC_BATCH * n_pages * 5) // 4

    def nrm(k, shape, scale):
        return jax.random.normal(k, shape, F32) * scale

    def gain(k, shape):
        return 1.0 + 0.1 * jax.random.normal(k, shape, F32)

    page_table = jax.random.permutation(kk[4], n_pool_pages)[: DEC_BATCH * n_pages]
    page_table = page_table.reshape(DEC_BATCH, n_pages).astype(jnp.int32)
    return {
        "x_prompt": nrm(kk[0], (BATCH, SEQ, D_MODEL), 1.0),
        "x_sample": nrm(kk[1], (DEC_BATCH, DEC_SEQ, D_MODEL), 1.0),
        "cache_k": nrm(kk[2], (DEPTH, n_pool_pages, PAGE_SIZE, N_HEADS, HEAD_DIM), 1.0),
        "cache_v": nrm(kk[3], (DEPTH, n_pool_pages, PAGE_SIZE, N_HEADS, HEAD_DIM), 1.0),
        "page_table": page_table,
        "state_conv": nrm(kk[5], (DEPTH, DEC_BATCH, CONV_WIDTH - 1, D_CONV), 0.5),
        "state_pool": nrm(kk[6], (DEPTH, DEC_BATCH, POOL_STATE, D_POOL), 1.0),
        "norm_mix": gain(kk[7], (DEPTH, D_MODEL)),
        "norm_ffn": gain(kk[8], (DEPTH, D_MODEL)),
        "norm_final": gain(kk[9], (D_MODEL,)),
        "w_in": nrm(kk[10], (DEPTH, D_MODEL, D_IN), D_MODEL ** -0.5),
        "w_attn_out": nrm(kk[11], (DEPTH, D_ATTN, D_MODEL), D_ATTN ** -0.5),
        "conv_w": nrm(kk[12], (DEPTH, CONV_WIDTH, D_CONV), CONV_WIDTH ** -0.5),
        "conv_b": nrm(kk[13], (DEPTH, D_CONV), 0.02),
        "conv_ln_g": gain(kk[14], (DEPTH, D_CONV)),
        "conv_ln_b": nrm(kk[15], (DEPTH, D_CONV), 0.02),
        "w_conv_out": nrm(kk[16], (DEPTH, D_CONV, D_MODEL), D_CONV ** -0.5),
        "pool_lin": nrm(kk[17], (DEPTH, N_POOL_GROUPS, POOL_GROUP, POOL_GROUP), POOL_GROUP ** -0.5),
        "pool_scale": gain(kk[18], (DEPTH, D_POOL)),
        "w_pool_out": nrm(kk[19], (DEPTH, D_POOL, D_MODEL), D_POOL ** -0.5),
        "w_out": nrm(kk[20], (DEPTH, D_MODEL, D_MODEL), D_MODEL ** -0.5),
        "ffn_w1": nrm(kk[21], (N_DENSE, D_MODEL, D_FF), D_MODEL ** -0.5),
        "ffn_w3": nrm(kk[22], (N_DENSE, D_MODEL, D_FF), D_MODEL ** -0.5),
        "ffn_w2": nrm(kk[23], (N_DENSE, D_FF, D_MODEL), D_FF ** -0.5),
        "moe_router": nrm(kk[24], (N_MOE, D_MODEL, N_EXPERTS), D_MODEL ** -0.5),
        "moe_w1": nrm(kk[25], (N_MOE, N_EXPERTS, D_MODEL, D_FF_EXPERT), D_MODEL ** -0.5),
        "moe_w3": nrm(kk[26], (N_MOE, N_EXPERTS, D_MODEL, D_FF_EXPERT), D_MODEL ** -0.5),
        "moe_w2": nrm(kk[27], (N_MOE, N_EXPERTS, D_FF_EXPERT, D_MODEL), D_FF_EXPERT ** -0.5),
    }


def reference(x_prompt, x_sample, cache_k, cache_v, page_table, state_conv, state_pool,
              norm_mix, norm_ffn, norm_final, w_in, w_attn_out, conv_w, conv_b, conv_ln_g, conv_ln_b,
              w_conv_out, pool_lin, pool_scale, w_pool_out, w_out,
              ffn_w1, ffn_w3, ffn_w2, moe_router, moe_w1, moe_w3, moe_w2):
    weights = (norm_mix, norm_ffn, norm_final, w_in, w_attn_out, conv_w, conv_b, conv_ln_g, conv_ln_b,
               w_conv_out, pool_lin, pool_scale, w_pool_out, w_out,
               ffn_w1, ffn_w3, ffn_w2, moe_router, moe_w1, moe_w3, moe_w2)
    past_len = page_table.shape[1] * cache_k.shape[2]
    pos_prompt = jnp.arange(x_prompt.shape[1], dtype=jnp.int32)
    pos_sample = past_len + jnp.arange(x_sample.shape[1], dtype=jnp.int32)
    y_prompt, k_p, v_p, conv_p, pool_p = forward(
        x_prompt, pos_prompt, None, None, None, None, None, *weights)
    y_sample, k_s, v_s, conv_s, pool_s = forward(
        x_sample, pos_sample, cache_k, cache_v, page_table, state_conv, state_pool, *weights)
    return (y_prompt, y_sample, k_p, v_p, k_s, v_s, conv_p, conv_s, pool_p, pool_s)
```

```python
import functools
import math

import jax
import jax.numpy as jnp
import numpy as np
from jax import lax
from jax.experimental import pallas as pl
from jax.experimental.pallas import tpu as pltpu

F32 = jnp.float32
BF16 = jnp.bfloat16

N_HEADS = 8
HEAD_DIM = 64
D_ATTN = N_HEADS * HEAD_DIM
ROPE_DIM = HEAD_DIM // 4
ROPE_THETA = 500000.0
MOBA_BLOCK = 256
MOBA_TOPK = 3
D_CONV = 512
CONV_WIDTH = 31
D_POOL = 512
POOL_WINDOWS = (2, 4, 8, 16)
POOL_GROUP = D_POOL // len(POOL_WINDOWS)
POOL_STATE = max(POOL_WINDOWS) - 1
N_BRANCHES = 3
N_EXPERTS = 8
EPS = 1e-6

LANES = 128
SLAB = 2 * HEAD_DIM
CONV_HALO = 32
POOL_HALO = 16
NEG = -1e30
VMEM_LIMIT = 56 * 1024 * 1024


def _cparams(sem):
    return pltpu.CompilerParams(dimension_semantics=sem, vmem_limit_bytes=VMEM_LIMIT)


def _resident(shape, index_map):
    return pl.BlockSpec(shape, index_map, pipeline_mode=pl.Buffered(1))


def _rms(x, gain):
    ms = jnp.mean(x * x, axis=-1, keepdims=True)
    return x * lax.rsqrt(ms + EPS) * gain


def _dot(a, b):
    return jnp.dot(a, b, preferred_element_type=F32)


def _dot_t(a, b):
    return lax.dot_general(a, b, (((1,), (1,)), ((), ())), preferred_element_type=F32)


_C_Q = 0
_C_K = _C_Q + N_HEADS * SLAB
_C_V = _C_K + N_HEADS * SLAB
_C_GLU = _C_V + D_ATTN
_C_PZ = _C_GLU + 2 * D_CONV
_C_G = _C_PZ + D_POOL


def _slab_weights(w):
    d = w.shape[0]
    w = w.reshape(d, N_HEADS, HEAD_DIM)
    z = jnp.zeros_like(w)
    even = jnp.concatenate([w, z], axis=-1)
    odd = jnp.concatenate([z, w], axis=-1)
    is_even = (jnp.arange(N_HEADS) % 2 == 0)[None, :, None]
    return jnp.where(is_even, even, odd).reshape(d, N_HEADS * SLAB)


def _prep_w_in(w_in):
    wq = w_in[:, :D_ATTN] * (HEAD_DIM ** -0.5)
    wk = w_in[:, D_ATTN:2 * D_ATTN]
    rest = w_in[:, 2 * D_ATTN:]
    return jnp.concatenate([_slab_weights(wq), _slab_weights(wk), rest], axis=1).astype(BF16)


def _rope_tables(pos):
    half = ROPE_DIM // 2
    step = -2.0 * math.log(ROPE_THETA) / ROPE_DIM
    step_hi = float(np.float32(step))
    step_lo = float(np.float32(step - step_hi))
    idx = jnp.arange(half, dtype=F32)
    inv_freq = jnp.exp(idx * step_hi + idx * step_lo)
    ang = pos.astype(F32)[:, None] * inv_freq[None, :]
    cos, sin = jnp.cos(ang), jnp.sin(ang)
    n = pos.shape[0]
    ones = jnp.ones((n, HEAD_DIM - ROPE_DIM), F32)
    zeros8 = jnp.zeros((n, half), F32)
    zeros48 = jnp.zeros((n, HEAD_DIM - ROPE_DIM), F32)
    c = jnp.concatenate([cos, cos, ones], axis=1)
    s_up = jnp.concatenate([zeros8, sin, zeros48], axis=1)
    s_dn = jnp.concatenate([-sin, zeros8, zeros48], axis=1)
    tile2 = lambda t: jnp.concatenate([t, t], axis=1)
    return tile2(c), tile2(s_up), tile2(s_dn)


def _inproj_kernel(x_ref, gain_ref, w_ref, cos_ref, sup_ref, sdn_ref, *out_refs, prompt, tiles_per_seq):
    if prompt:
        qa_ref, ka_ref, k_ref, v_ref, vb_ref, km_ref, u_ref, pz_ref, g_ref = out_refs
    else:
        qa_ref, k_ref, v_ref, u_ref, pz_ref, g_ref = out_refs
    tm = x_ref.shape[0]
    h = _rms(x_ref[...], gain_ref[...]).astype(BF16)
    cosv, sup, sdn = cos_ref[...], sup_ref[...], sdn_ref[...]

    def rope(slab):
        return slab * cosv + pltpu.roll(slab, ROPE_DIM // 2, 1) * sup + pltpu.roll(slab, LANES - ROPE_DIM // 2, 1) * sdn

    q_all = _dot(h, w_ref[:, _C_Q:_C_K])
    for hd in range(N_HEADS):
        qa_ref[:, hd * SLAB:(hd + 1) * SLAB] = rope(q_all[:, hd * SLAB:(hd + 1) * SLAB]).astype(BF16)

    k_all = _dot(h, w_ref[:, _C_K:_C_V])
    lane = lax.broadcasted_iota(jnp.int32, (tm, LANES), 1)
    if prompt:
        row = lax.broadcasted_iota(jnp.int32, (tm, LANES), 0)
        blk = ((pl.program_id(0) % tiles_per_seq) * tm + row) // MOBA_BLOCK
        onehot_even = (lane == blk + HEAD_DIM).astype(F32)
        onehot_odd = (lane == blk).astype(F32)
    for pair in range(N_HEADS // 2):
        k_even = rope(k_all[:, (2 * pair) * SLAB:(2 * pair + 1) * SLAB])
        k_odd = rope(k_all[:, (2 * pair + 1) * SLAB:(2 * pair + 2) * SLAB])
        k_ref[:, pair * LANES:(pair + 1) * LANES] = jnp.where(lane < HEAD_DIM, k_even, k_odd)
        if prompt:
            for hd, k_slab, onehot in ((2 * pair, k_even, onehot_even), (2 * pair + 1, k_odd, onehot_odd)):
                ka_ref[:, hd * SLAB:(hd + 1) * SLAB] = (k_slab + onehot).astype(BF16)
                km = jnp.mean(k_slab.reshape(tm // MOBA_BLOCK, MOBA_BLOCK, SLAB), axis=1)
                km_ref[:, 0, hd * SLAB:(hd + 1) * SLAB] = km

    v_all = _dot(h, w_ref[:, _C_V:_C_GLU])
    v_ref[...] = v_all
    if prompt:
        vb_ref[...] = v_all.astype(BF16)

    glu = _dot(h, w_ref[:, _C_GLU:_C_PZ])
    u_ref[...] = glu[:, :D_CONV] * jax.nn.sigmoid(glu[:, D_CONV:])
    pz_ref[...] = _dot(h, w_ref[:, _C_PZ:_C_G])
    g_ref[...] = jax.nn.sigmoid(_dot(h, w_ref[:, _C_G:]))


def _inproj(x2d, gain, w_aug, tables, *, prompt, seq_len, tm):
    t, d_model = x2d.shape
    n_tiles = t // tm
    cos_t, sup_t, sdn_t = tables
    n_tab = cos_t.shape[0] // tm
    tiles_per_seq = max(seq_len // tm, 1)
    n_cols = w_aug.shape[1]
    row = lambda c: pl.BlockSpec((tm, c), lambda i: (i, 0))
    tab = pl.BlockSpec((tm, LANES), lambda i: (i % n_tab, 0))
    f = jax.ShapeDtypeStruct
    outs = [(f((t, N_HEADS * SLAB), BF16), row(N_HEADS * SLAB))]
    if prompt:
        outs.append((f((t, N_HEADS * SLAB), BF16), row(N_HEADS * SLAB)))
    outs += [(f((t, D_ATTN), F32), row(D_ATTN)), (f((t, D_ATTN), F32), row(D_ATTN))]
    if prompt:
        bpt = tm // MOBA_BLOCK
        outs.append((f((t, D_ATTN), BF16), row(D_ATTN)))
        outs.append((f((t // MOBA_BLOCK, 1, N_HEADS * SLAB), F32),
                     pl.BlockSpec((bpt, 1, N_HEADS * SLAB), lambda i: (i, 0, 0))))
    outs += [(f((t, D_CONV), F32), row(D_CONV)), (f((t, D_POOL), F32), row(D_POOL)),
             (f((t, N_BRANCHES * d_model), F32), row(N_BRANCHES * d_model))]
    return pl.pallas_call(
        functools.partial(_inproj_kernel, prompt=prompt, tiles_per_seq=tiles_per_seq),
        grid=(n_tiles,),
        in_specs=[row(d_model), _resident((1, d_model), lambda i: (0, 0)),
                  _resident((d_model, n_cols), lambda i: (0, 0)), tab, tab, tab],
        out_specs=[o[1] for o in outs],
        out_shape=[o[0] for o in outs],
        compiler_params=_cparams(("parallel",)),
    )(x2d, gain.reshape(1, d_model), w_aug, cos_t, sup_t, sdn_t)


def _top_blocks(gate, valid, idx, axis):
    g = jnp.where(valid, gate, -jnp.inf)
    picked = jnp.zeros(gate.shape, jnp.bool_)
    for _ in range(MOBA_TOPK):
        m = jnp.max(g, axis=axis, keepdims=True)
        first = jnp.min(jnp.where(g == m, idx, jnp.int32(1 << 20)), axis=axis, keepdims=True)
        pick = (idx == first) & (m > -jnp.inf)
        picked = picked | pick
        g = jnp.where(pick, -jnp.inf, g)
    return picked


def _attn_prompt_kernel(q_ref, k_ref, v_ref, km_ref, o_ref):
    qb = pl.program_id(1)
    tq = q_ref.shape[0]
    nb = km_ref.shape[0]
    lane = lax.broadcasted_iota(jnp.int32, (tq, LANES), 1)
    blk_row = lax.broadcasted_iota(jnp.int32, (nb, tq), 0)
    row = lax.broadcasted_iota(jnp.int32, (tq, tq), 0)
    col = lax.broadcasted_iota(jnp.int32, (tq, tq), 1)

    q_past, q_own = [], []
    for hd in range(N_HEADS):
        qs = q_ref[:, hd * SLAB:(hd + 1) * SLAB]
        gate_t = _dot_t(km_ref[:, hd * SLAB:(hd + 1) * SLAB].astype(BF16), qs)
        picked = _top_blocks(gate_t, blk_row < qb, blk_row, 0)
        pen_t = jnp.where(picked, 0.0, NEG)
        pieces = [pen_t]
        if nb < HEAD_DIM:
            pieces.append(jnp.full((HEAD_DIM - nb, tq), NEG, F32))
        zeros = jnp.zeros((HEAD_DIM, tq), F32)
        pieces = [zeros] + pieces if hd % 2 == 0 else pieces + [zeros]
        pen = jnp.concatenate(pieces, axis=0).T
        spare_blk = lane - (HEAD_DIM if hd % 2 == 0 else 0)
        qf = qs.astype(F32) + pen
        q_past.append(qf.astype(BF16))
        q_own.append(jnp.where(spare_blk == qb, 0.0, qf).astype(BF16))

    def sweep(start, size, qs, carry, causal):
        start = pl.multiple_of(start, MOBA_BLOCK)
        new = []
        for pair in range(N_HEADS // 2):
            vb = v_ref[pl.ds(start, size), pair * LANES:(pair + 1) * LANES]
            upd = []
            for hd in (2 * pair, 2 * pair + 1):
                s = _dot_t(qs[hd], k_ref[pl.ds(start, size), hd * SLAB:(hd + 1) * SLAB])
                if causal:
                    s = jnp.where(col <= row, s, NEG)
                s_max = jnp.max(s, axis=1, keepdims=True)
                if carry is None:
                    m_new, alpha = s_max, None
                else:
                    m_old, l_old = carry[3 * pair + (hd % 2)], carry[3 * pair + 2][hd % 2]
                    m_new = jnp.maximum(m_old, s_max)
                    alpha = jnp.exp(m_old - m_new)
                p = jnp.exp(s - m_new)
                l_new = jnp.sum(p, axis=1, keepdims=True)
                if alpha is not None:
                    l_new = alpha * l_old + l_new
                upd.append((m_new, alpha, l_new, _dot(p.astype(BF16), vb)))
            pv = jnp.where(lane < HEAD_DIM, upd[0][3], upd[1][3])
            if carry is None:
                acc = pv
            else:
                acc = jnp.where(lane < HEAD_DIM, upd[0][1], upd[1][1]) * carry[3 * pair + 2][2] + pv
            new += [upd[0][0], upd[1][0], (upd[0][2], upd[1][2], acc)]
        return tuple(new)

    state = sweep(qb * MOBA_BLOCK, MOBA_BLOCK, q_own, None, True)
    state = lax.fori_loop(0, (qb + 1) // 2,
                          lambda i, c: sweep(i * (2 * MOBA_BLOCK), 2 * MOBA_BLOCK, q_past, c, False), state)
    for pair in range(N_HEADS // 2):
        l0, l1, acc = state[3 * pair + 2]
        o_ref[:, pair * LANES:(pair + 1) * LANES] = (acc / jnp.where(lane < HEAD_DIM, l0, l1)).astype(o_ref.dtype)


def _attn_prompt(q_aug, k_aug, v_bf, kmean, *, n_seq, seq_len):
    t = q_aug.shape[0]
    nqb = seq_len // MOBA_BLOCK
    assert nqb % 8 == 0 or nqb == seq_len // MOBA_BLOCK
    km = kmean.reshape(n_seq, nqb, N_HEADS * SLAB)
    return pl.pallas_call(
        _attn_prompt_kernel,
        grid=(n_seq, nqb),
        in_specs=[
            pl.BlockSpec((MOBA_BLOCK, N_HEADS * SLAB), lambda b, i: (b * nqb + i, 0)),
            _resident((seq_len, N_HEADS * SLAB), lambda b, i: (b, 0)),
            _resident((seq_len, D_ATTN), lambda b, i: (b, 0)),
            pl.BlockSpec((None, nqb, N_HEADS * SLAB), lambda b, i: (b, 0, 0)),
        ],
        out_specs=pl.BlockSpec((MOBA_BLOCK, D_ATTN), lambda b, i: (b * nqb + i, 0)),
        out_shape=jax.ShapeDtypeStruct((t, D_ATTN), BF16),
        compiler_params=_cparams(("parallel", "arbitrary")),
    )(q_aug, k_aug, v_bf, km)


def _attn_sample_kernel(pt_ref, q_ref, kn_ref, vn_ref, e_ref, *rest, pages_per_step, n_steps, page_size):
    del pt_ref
    k_refs = rest[:pages_per_step]
    v_refs = rest[pages_per_step:2 * pages_per_step]
    o_ref = rest[2 * pages_per_step]
    qm_scr, s_scr, p_scr, acc_scr, l_scr = rest[2 * pages_per_step + 1:]
    j = pl.program_id(1)
    ls = q_ref.shape[0]
    rows = N_HEADS * ls
    lane_d = lax.broadcasted_iota(jnp.int32, (ls, D_ATTN), 1)

    @pl.when(j == 0)
    def _():
        qa = q_ref[...]
        lane = lax.broadcasted_iota(jnp.int32, (ls, LANES), 1)
        q_nat = jnp.concatenate(
            [jnp.where(lane < HEAD_DIM, qa[:, (2 * p) * SLAB:(2 * p + 1) * SLAB],
                       qa[:, (2 * p + 1) * SLAB:(2 * p + 2) * SLAB]) for p in range(N_HEADS // 2)], axis=1)
        qm_scr[...] = jnp.concatenate(
            [jnp.where(lane_d // HEAD_DIM == hd, q_nat, jnp.zeros_like(q_nat)) for hd in range(N_HEADS)], axis=0)

    @pl.when(j < n_steps)
    def _():
        qm = qm_scr[...]
        for i in range(pages_per_step):
            col = pl.multiple_of((j * pages_per_step + i) * page_size, page_size)
            s_scr[:, pl.ds(col, page_size)] = _dot(qm, k_refs[i][...].astype(BF16))

    @pl.when(j == n_steps - 1)
    def _():
        qm = qm_scr[...]
        n_past = n_steps * pages_per_step * page_size // MOBA_BLOCK
        lane = lax.broadcasted_iota(jnp.int32, (rows, LANES), 1)
        s_all = s_scr[...]
        expand = e_ref[...]
        gate = lax.dot_general(s_all, expand, (((1,), (1,)), ((), ())), preferred_element_type=F32,
                               precision=lax.Precision.HIGHEST) * (1.0 / MOBA_BLOCK)
        picked = _top_blocks(gate, lane < n_past, lane, 1)
        sel = jnp.dot(jnp.where(picked, 1.0, 0.0), expand, preferred_element_type=F32)
        s = jnp.where(sel > 0.5, s_all, NEG)
        pad = jnp.zeros((LANES - ls, D_ATTN), BF16)
        s_own = _dot_t(qm, jnp.concatenate([kn_ref[...].astype(BF16), pad], axis=0))
        qi = lax.broadcasted_iota(jnp.int32, (rows, LANES), 0) % ls
        s_own = jnp.where(lane <= qi, s_own, NEG)
        m = jnp.maximum(jnp.max(s, axis=1, keepdims=True), jnp.max(s_own, axis=1, keepdims=True))
        p = jnp.exp(s - m)
        p_own = jnp.exp(s_own - m)
        l_scr[...] = jnp.sum(p, axis=1, keepdims=True) + jnp.sum(p_own, axis=1, keepdims=True)
        p_scr[...] = p.astype(BF16)
        acc_scr[...] = _dot(p_own.astype(BF16), jnp.concatenate([vn_ref[...].astype(BF16), pad], axis=0))

    @pl.when(j >= n_steps)
    def _():
        acc = acc_scr[...]
        for i in range(pages_per_step):
            col = pl.multiple_of(((j - n_steps) * pages_per_step + i) * page_size, page_size)
            acc = acc + _dot_t(p_scr[:, pl.ds(col, page_size)], v_refs[i][...].astype(BF16))
        acc_scr[...] = acc

    @pl.when(j == 2 * n_steps - 1)
    def _():
        o = acc_scr[...] / l_scr[...]
        out = jnp.zeros((ls, D_ATTN), F32)
        for hd in range(N_HEADS):
            out = out + jnp.where(lane_d // HEAD_DIM == hd, o[hd * ls:(hd + 1) * ls, :], 0.0)
        o_ref[...] = out.astype(o_ref.dtype)


def _attn_sample(q_aug, k_new, v_new, cache_kt, cache_vt, page_table, layer, *, n_seq, seq_len):
    n_pages = page_table.shape[1]
    page_size = cache_kt.shape[3]
    past_len = n_pages * page_size
    assert past_len % MOBA_BLOCK == 0 and MOBA_BLOCK % page_size == 0 and seq_len % 8 == 0
    assert seq_len <= MOBA_BLOCK and past_len // MOBA_BLOCK <= LANES
    pages_per_step = 8 if n_pages % 8 == 0 else MOBA_BLOCK // page_size
    n_steps = n_pages // pages_per_step
    rows = N_HEADS * seq_len
    expand = (jnp.arange(past_len)[None, :] // MOBA_BLOCK == jnp.arange(LANES)[:, None]).astype(F32)

    def k_map(i):
        return lambda b, j, pt: (layer, pt[b, jnp.minimum(j, n_steps - 1) * pages_per_step + i], 0, 0)

    def v_map(i):
        return lambda b, j, pt: (layer, pt[b, jnp.maximum(j - n_steps, 0) * pages_per_step + i], 0, 0)

    page = lambda m: pl.BlockSpec((None, None, D_ATTN, page_size), m)
    grid_spec = pltpu.PrefetchScalarGridSpec(
        num_scalar_prefetch=1,
        grid=(n_seq, 2 * n_steps),
        in_specs=[pl.BlockSpec((seq_len, N_HEADS * SLAB), lambda b, j, pt: (b, 0)),
                  pl.BlockSpec((seq_len, D_ATTN), lambda b, j, pt: (b, 0)),
                  pl.BlockSpec((seq_len, D_ATTN), lambda b, j, pt: (b, 0)),
                  _resident((LANES, past_len), lambda b, j, pt: (0, 0))]
                 + [page(k_map(i)) for i in range(pages_per_step)]
                 + [page(v_map(i)) for i in range(pages_per_step)],
        out_specs=pl.BlockSpec((seq_len, D_ATTN), lambda b, j, pt: (b, 0)),
        scratch_shapes=[pltpu.VMEM((rows, D_ATTN), BF16), pltpu.VMEM((rows, past_len), F32),
                        pltpu.VMEM((rows, past_len), BF16), pltpu.VMEM((rows, D_ATTN), F32),
                        pltpu.VMEM((rows, 1), F32)],
    )
    return pl.pallas_call(
        functools.partial(_attn_sample_kernel, pages_per_step=pages_per_step, n_steps=n_steps, page_size=page_size),
        grid_spec=grid_spec,
        out_shape=jax.ShapeDtypeStruct((n_seq * seq_len, D_ATTN), BF16),
        compiler_params=_cparams(("parallel", "arbitrary")),
    )(page_table, q_aug, k_new, v_new, expand, *([cache_kt] * pages_per_step), *([cache_vt] * pages_per_step))


def _transposed_pages(cache):
    depth, n_pool, page_size = cache.shape[:3]
    return jnp.transpose(cache, (0, 1, 3, 4, 2)).reshape(depth, n_pool, D_ATTN, page_size)


def _mix_kernel(x_ref, u_ref, uh_ref, z_ref, zh_ref, a_ref, g_ref,
                cw_ref, cb_ref, lg_ref, lb_ref, wc_ref, pl_ref, ps_ref, wp_ref, wa_ref, wo_ref,
                o_ref, ext_ref, zext_ref, *, prompt, n_pool_prev):
    tl = u_ref.shape[0]
    i = pl.program_id(1)
    d_model = x_ref.shape[1]

    uh, zh = uh_ref[...], zh_ref[...]
    if prompt:
        uh = jnp.where(i == 0, 0.0, uh)
        zh = jnp.where(i == 0, 0.0, zh)
    ext_ref[0:CONV_HALO, :] = uh
    ext_ref[CONV_HALO:, :] = u_ref[...]
    zext_ref[0:POOL_HALO, :] = zh
    z = z_ref[...]
    zext_ref[POOL_HALO:, :] = z

    y = jnp.zeros((tl, D_CONV), F32) + cb_ref[...]
    base = CONV_HALO - (CONV_WIDTH - 1)
    for j in range(CONV_WIDTH):
        y = y + ext_ref[base + j:base + j + tl, :] * cw_ref[j:j + 1, :]
    mu = jnp.mean(y, axis=-1, keepdims=True)
    yc = y - mu
    var = jnp.mean(yc * yc, axis=-1, keepdims=True)
    yn = yc * lax.rsqrt(var + EPS) * lg_ref[...] + lb_ref[...]
    c = _dot(jax.nn.silu(yn).astype(BF16), wc_ref[...])

    pos = i * tl + lax.broadcasted_iota(jnp.int32, (tl, 1), 0)
    parts = []
    for gi, w in enumerate(POOL_WINDOWS):
        lo, hi = gi * POOL_GROUP, (gi + 1) * POOL_GROUP
        tot = z[:, lo:hi]
        for back in range(1, w):
            tot = tot + zext_ref[POOL_HALO - back:POOL_HALO - back + tl, lo:hi]
        cnt = jnp.minimum(w, pos + 1 + n_pool_prev).astype(F32)
        d = tot / cnt - z[:, lo:hi]
        parts.append(_dot(d.astype(BF16), pl_ref[gi]))
    pm = jnp.concatenate(parts, axis=1) * ps_ref[...]
    p = _dot(pm.astype(BF16), wp_ref[...])

    a = _dot(a_ref[...], wa_ref[...])
    merged = (g_ref[:, 0:d_model] * a + g_ref[:, d_model:2 * d_model] * c
              + g_ref[:, 2 * d_model:3 * d_model] * p)
    o_ref[...] = x_ref[...] + _dot(merged.astype(BF16), wo_ref[...])


def _mix(x, u, z, attn, gates, conv_hist, pool_hist, lw, *, prompt, tl):
    n_seq, seq_len, d_model = x.shape
    n_tiles = seq_len // tl
    seq = lambda c: pl.BlockSpec((None, tl, c), lambda b, i: (b, i, 0))
    if prompt:
        uh_arr, zh_arr = u, z
        uh_spec = pl.BlockSpec((None, CONV_HALO, D_CONV),
                               lambda b, i: (b, jnp.maximum(i * (tl // CONV_HALO) - 1, 0), 0))
        zh_spec = pl.BlockSpec((None, POOL_HALO, D_POOL),
                               lambda b, i: (b, jnp.maximum(i * (tl // POOL_HALO) - 1, 0), 0))
        n_pool_prev = 0
    else:
        assert n_tiles == 1
        uh_arr = jnp.pad(conv_hist, ((0, 0), (CONV_HALO - conv_hist.shape[1], 0), (0, 0)))
        zh_arr = jnp.pad(pool_hist, ((0, 0), (POOL_HALO - pool_hist.shape[1], 0), (0, 0)))
        uh_spec = pl.BlockSpec((None, CONV_HALO, D_CONV), lambda b, i: (b, 0, 0))
        zh_spec = pl.BlockSpec((None, POOL_HALO, D_POOL), lambda b, i: (b, 0, 0))
        n_pool_prev = POOL_STATE
    const = lambda shape: _resident(shape, lambda b, i: (0,) * len(shape))
    return pl.pallas_call(
        functools.partial(_mix_kernel, prompt=prompt, n_pool_prev=n_pool_prev),
        grid=(n_seq, n_tiles),
        in_specs=[seq(d_model), seq(D_CONV), uh_spec, seq(D_POOL), zh_spec, seq(D_ATTN), seq(N_BRANCHES * d_model),
                  const((CONV_WIDTH, D_CONV)), const((1, D_CONV)), const((1, D_CONV)), const((1, D_CONV)),
                  const((D_CONV, d_model)), const((len(POOL_WINDOWS), POOL_GROUP, POOL_GROUP)),
                  const((1, D_POOL)), const((D_POOL, d_model)), const((D_ATTN, d_model)),
                  const((d_model, d_model))],
        out_specs=seq(d_model),
        out_shape=jax.ShapeDtypeStruct(x.shape, F32),
        scratch_shapes=[pltpu.VMEM((CONV_HALO + tl, D_CONV), F32), pltpu.VMEM((POOL_HALO + tl, D_POOL), F32)],
        compiler_params=_cparams(("parallel", "arbitrary")),
    )(x, u, uh_arr, z, zh_arr, attn, gates,
      lw["conv_w"], lw["conv_b"], lw["conv_ln_g"], lw["conv_ln_b"], lw["w_conv_out"], lw["pool_lin"],
      lw["pool_scale"], lw["w_pool_out"], lw["w_attn_out"], lw["w_out"])


def _ffn_chunk(d_ff):
    for n in (2, 4, 1):
        if d_ff % (n * LANES) == 0 and d_ff // n <= 2048:
            return d_ff // n
    return d_ff


def _ffn_kernel(x_ref, gain_ref, w1_ref, w3_ref, w2_ref, fg_ref, o_ref, *, chunk, final_norm):
    x = x_ref[...]
    h = _rms(x, gain_ref[...]).astype(BF16)
    acc = x
    for c0 in range(0, w1_ref.shape[1], chunk):
        act = jax.nn.silu(_dot(h, w1_ref[:, c0:c0 + chunk])) * _dot(h, w3_ref[:, c0:c0 + chunk])
        acc = acc + _dot(act.astype(BF16), w2_ref[c0:c0 + chunk, :])
    o_ref[...] = _rms(acc, fg_ref[...]) if final_norm else acc


def _ffn(x2d, gain, w1, w3, w2, final_gain, *, tm):
    t, d_model = x2d.shape
    d_ff = w1.shape[1]
    final_norm = final_gain is not None
    fg = (final_gain if final_norm else gain).reshape(1, d_model)
    row = pl.BlockSpec((tm, d_model), lambda i: (i, 0))
    vec = _resident((1, d_model), lambda i: (0, 0))
    return pl.pallas_call(
        functools.partial(_ffn_kernel, chunk=_ffn_chunk(d_ff), final_norm=final_norm),
        grid=(t // tm,),
        in_specs=[row, vec, _resident((d_model, d_ff), lambda i: (0, 0)), _resident((d_model, d_ff), lambda i: (0, 0)),
                  _resident((d_ff, d_model), lambda i: (0, 0)), vec],
        out_specs=row,
        out_shape=jax.ShapeDtypeStruct((t, d_model), F32),
        compiler_params=_cparams(("parallel",)),
    )(x2d, gain.reshape(1, d_model), w1, w3, w2, fg)


def _moe_kernel(x_ref, gain_ref, r_ref, w1_ref, w3_ref, w2_ref, fg_ref, o_ref,
                h_scr, comb_scr, acc_scr, y_scr, *, final_norm):
    e, f = pl.program_id(1), pl.program_id(2)
    n_e, n_f = pl.num_programs(1), pl.num_programs(2)
    tm = x_ref.shape[0]
    lane = lax.broadcasted_iota(jnp.int32, (tm, LANES), 1)

    @pl.when((e == 0) & (f == 0))
    def _():
        h32 = _rms(x_ref[...], gain_ref[...])
        h_scr[...] = h32.astype(BF16)
        logits = jnp.dot(h32, r_ref[...], preferred_element_type=F32, precision=lax.Precision.HIGHEST)
        logits = jnp.where(lane < N_EXPERTS, logits, -jnp.inf)
        m1 = jnp.max(logits, axis=1, keepdims=True)
        i1 = jnp.min(jnp.where(logits == m1, lane, LANES), axis=1, keepdims=True)
        rest = jnp.where(lane == i1, -jnp.inf, logits)
        m2 = jnp.max(rest, axis=1, keepdims=True)
        i2 = jnp.min(jnp.where(rest == m2, lane, LANES), axis=1, keepdims=True)
        e2 = jnp.exp(m2 - m1)
        den = 1.0 + e2
        comb_scr[...] = jnp.where(lane == i1, 1.0 / den, 0.0) + jnp.where(lane == i2, e2 / den, 0.0)
        y_scr[...] = jnp.zeros_like(y_scr)

    h = h_scr[...]
    act = jax.nn.silu(_dot(h, w1_ref[...])) * _dot(h, w3_ref[...])
    part = _dot(act.astype(BF16), w2_ref[...])

    @pl.when(f == 0)
    def _():
        acc_scr[...] = part

    @pl.when(f > 0)
    def _():
        acc_scr[...] += part

    @pl.when(f == n_f - 1)
    def _():
        w_e = jnp.sum(jnp.where(lane == e, comb_scr[...], 0.0), axis=1, keepdims=True)
        y_scr[...] += w_e * acc_scr[...]

    @pl.when((e == n_e - 1) & (f == n_f - 1))
    def _():
        out = x_ref[...] + y_scr[...]
        o_ref[...] = _rms(out, fg_ref[...]) if final_norm else out


def _moe(x2d, gain, router, w1, w3, w2, final_gain, *, tm):
    t, d_model = x2d.shape
    n_e, _, d_ff = w1.shape
    chunk = _ffn_chunk(d_ff)
    n_f = d_ff // chunk
    final_norm = final_gain is not None
    fg = (final_gain if final_norm else gain).reshape(1, d_model)
    r_pad = jnp.pad(router, ((0, 0), (0, LANES - n_e)))
    row = pl.BlockSpec((tm, d_model), lambda i, e, f: (i, 0))
    vec = _resident((1, d_model), lambda i, e, f: (0, 0))
    return pl.pallas_call(
        functools.partial(_moe_kernel, final_norm=final_norm),
        grid=(t // tm, n_e, n_f),
        in_specs=[row, vec, _resident((d_model, LANES), lambda i, e, f: (0, 0)),
                  pl.BlockSpec((None, d_model, chunk), lambda i, e, f: (e, 0, f)),
                  pl.BlockSpec((None, d_model, chunk), lambda i, e, f: (e, 0, f)),
                  pl.BlockSpec((None, chunk, d_model), lambda i, e, f: (e, f, 0)), vec],
        out_specs=row,
        out_shape=jax.ShapeDtypeStruct((t, d_model), F32),
        scratch_shapes=[pltpu.VMEM((tm, d_model), BF16), pltpu.VMEM((tm, LANES), F32),
                        pltpu.VMEM((tm, d_model), F32), pltpu.VMEM((tm, d_model), F32)],
        compiler_params=_cparams(("parallel", "arbitrary", "arbitrary")),
    )(x2d, gain.reshape(1, d_model), r_pad, w1, w3, w2, fg)


def _pick_tile(n, prefs):
    for p in prefs:
        if n % p == 0:
            return p
    return n


def _forward(x, pos, cache, layers, norm_final, *, prompt):
    n_seq, seq_len, d_model = x.shape
    t = n_seq * seq_len
    depth = len(layers)
    if prompt:
        assert seq_len % MOBA_BLOCK == 0 and seq_len // MOBA_BLOCK <= HEAD_DIM
        tm_in = _pick_tile(seq_len, (512, MOBA_BLOCK))
        tables = _rope_tables(pos)
        tl = _pick_tile(seq_len, (512, 256, 128, 64, 32))
        tm_ffn = _pick_tile(t, (512, 256))
        tm_moe = _pick_tile(t, (512, 256))
    else:
        tm_in = t
        tables = tuple(jnp.tile(tb, (n_seq, 1)) for tb in _rope_tables(pos))
        tl = seq_len
        tm_ffn = tm_moe = t
        cache_kt, cache_vt, page_table, state_conv, state_pool = cache
    ks, vs, cs, ps = [], [], [], []
    for l, lw in enumerate(layers):
        outs = _inproj(x.reshape(t, d_model), lw["norm_mix"], lw["w_in_aug"], tables,
                       prompt=prompt, seq_len=seq_len, tm=tm_in)
        if prompt:
            q_aug, k_aug, k_new, v_new, v_bf, kmean, u, z, gates = outs
            attn = _attn_prompt(q_aug, k_aug, v_bf, kmean, n_seq=n_seq, seq_len=seq_len)
            conv_hist = pool_hist = None
        else:
            q_aug, k_new, v_new, u, z, gates = outs
            attn = _attn_sample(q_aug, k_new, v_new, cache_kt, cache_vt, page_table, l,
                                n_seq=n_seq, seq_len=seq_len)
            conv_hist, pool_hist = state_conv[l], state_pool[l]
        r3 = lambda a: a.reshape(n_seq, seq_len, a.shape[-1])
        u3, z3 = r3(u), r3(z)
        x = _mix(x, u3, z3, r3(attn), r3(gates), conv_hist, pool_hist, lw, prompt=prompt, tl=tl)
        final_gain = norm_final if l == depth - 1 else None
        x2d = x.reshape(t, d_model)
        if lw["kind"] == "dense":
            x2d = _ffn(x2d, lw["norm_ffn"], lw["w1"], lw["w3"], lw["w2"], final_gain, tm=tm_ffn)
        else:
            x2d = _moe(x2d, lw["norm_ffn"], lw["router"], lw["w1"], lw["w3"], lw["w2"], final_gain, tm=tm_moe)
        x = x2d.reshape(n_seq, seq_len, d_model)
        ks.append(k_new.reshape(n_seq, seq_len, N_HEADS, HEAD_DIM))
        vs.append(v_new.reshape(n_seq, seq_len, N_HEADS, HEAD_DIM))
        if prompt:
            cs.append(u3[:, seq_len - (CONV_WIDTH - 1):])
            ps.append(z3[:, seq_len - POOL_STATE:])
        else:
            cs.append(jnp.concatenate([conv_hist, u3], axis=1)[:, -(CONV_WIDTH - 1):])
            ps.append(jnp.concatenate([pool_hist, z3], axis=1)[:, -POOL_STATE:])
    return x, jnp.stack(ks), jnp.stack(vs), jnp.stack(cs), jnp.stack(ps)


def kernel(x_prompt, x_sample, cache_k, cache_v, page_table, state_conv, state_pool, norm_mix, norm_ffn, norm_final, w_in, w_attn_out, conv_w, conv_b, conv_ln_g, conv_ln_b, w_conv_out, pool_lin, pool_scale, w_pool_out, w_out, ffn_w1, ffn_w3, ffn_w2, moe_router, moe_w1, moe_w3, moe_w2):
    depth = norm_mix.shape[0]
    layers = []
    for l in range(depth):
        lw = dict(
            norm_mix=norm_mix[l], norm_ffn=norm_ffn[l],
            w_in_aug=_prep_w_in(w_in[l]),
            w_attn_out=w_attn_out[l].astype(BF16), conv_w=conv_w[l], conv_b=conv_b[l].reshape(1, -1),
            conv_ln_g=conv_ln_g[l].reshape(1, -1), conv_ln_b=conv_ln_b[l].reshape(1, -1),
            w_conv_out=w_conv_out[l].astype(BF16), pool_lin=pool_lin[l].astype(BF16),
            pool_scale=pool_scale[l].reshape(1, -1), w_pool_out=w_pool_out[l].astype(BF16),
            w_out=w_out[l].astype(BF16))
        i = l // 2
        if l % 2 == 0:
            lw.update(kind="dense", w1=ffn_w1[i].astype(BF16), w3=ffn_w3[i].astype(BF16), w2=ffn_w2[i].astype(BF16))
        else:
            lw.update(kind="moe", router=moe_router[i], w1=moe_w1[i].astype(BF16), w3=moe_w3[i].astype(BF16),
                      w2=moe_w2[i].astype(BF16))
        layers.append(lw)

    past_len = page_table.shape[1] * cache_k.shape[2]
    pos_prompt = jnp.arange(x_prompt.shape[1], dtype=jnp.int32)
    pos_sample = past_len + jnp.arange(x_sample.shape[1], dtype=jnp.int32)
    y_p, k_p, v_p, conv_p, pool_p = _forward(x_prompt, pos_prompt, None, layers, norm_final, prompt=True)
    cache = (_transposed_pages(cache_k), _transposed_pages(cache_v), page_table, state_conv, state_pool)
    y_s, k_s, v_s, conv_s, pool_s = _forward(x_sample, pos_sample, cache, layers, norm_final, prompt=False)
    return (y_p, y_s, k_p, v_p, k_s, v_s, conv_p, conv_s, pool_p, pool_s)
```

```python
import functools
import math

import jax
import jax.numpy as jnp
import numpy as np
from jax import lax
from jax.experimental import pallas as pl
from jax.experimental.pallas import tpu as pltpu

F32 = jnp.float32
BF16 = jnp.bfloat16

N_HEADS = 8
HEAD_DIM = 64
D_ATTN = N_HEADS * HEAD_DIM
ROPE_DIM = HEAD_DIM // 4
ROPE_THETA = 500000.0
MOBA_BLOCK = 256
MOBA_TOPK = 3
D_CONV = 512
CONV_WIDTH = 31
D_POOL = 512
POOL_WINDOWS = (2, 4, 8, 16)
POOL_GROUP = D_POOL // len(POOL_WINDOWS)
POOL_STATE = max(POOL_WINDOWS) - 1
N_BRANCHES = 3
N_EXPERTS = 8
EPS = 1e-6

LANES = 128
SLAB = 2 * HEAD_DIM
CONV_HALO = 32
POOL_HALO = 16
NEG = -1e30
VMEM_LIMIT = 56 * 1024 * 1024


def _cparams(sem):
    return pltpu.CompilerParams(dimension_semantics=sem, vmem_limit_bytes=VMEM_LIMIT)


def _resident(shape, index_map):
    return pl.BlockSpec(shape, index_map, pipeline_mode=pl.Buffered(1))


def _rms(x, gain):
    ms = jnp.mean(x * x, axis=-1, keepdims=True)
    return x * lax.rsqrt(ms + EPS) * gain


def _dot(a, b):
    return jnp.dot(a, b, preferred_element_type=F32)


def _dot_t(a, b):
    return lax.dot_general(a, b, (((1,), (1,)), ((), ())), preferred_element_type=F32)


_C_Q = 0
_C_K = _C_Q + N_HEADS * SLAB
_C_V = _C_K + N_HEADS * SLAB
_C_GLU = _C_V + D_ATTN
_C_PZ = _C_GLU + 2 * D_CONV
_C_G = _C_PZ + D_POOL


def _slab_weights(w):
    d = w.shape[0]
    w = w.reshape(d, N_HEADS, HEAD_DIM)
    z = jnp.zeros_like(w)
    even = jnp.concatenate([w, z], axis=-1)
    odd = jnp.concatenate([z, w], axis=-1)
    is_even = (jnp.arange(N_HEADS) % 2 == 0)[None, :, None]
    return jnp.where(is_even, even, odd).reshape(d, N_HEADS * SLAB)


def _prep_w_in(w_in):
    wq = w_in[:, :D_ATTN] * (HEAD_DIM ** -0.5)
    wk = w_in[:, D_ATTN:2 * D_ATTN]
    rest = w_in[:, 2 * D_ATTN:]
    return jnp.concatenate([_slab_weights(wq), _slab_weights(wk), rest], axis=1).astype(BF16)


def _rope_tables(pos):
    half = ROPE_DIM // 2
    step = -2.0 * math.log(ROPE_THETA) / ROPE_DIM
    step_hi = float(np.float32(step))
    step_lo = float(np.float32(step - step_hi))
    idx = jnp.arange(half, dtype=F32)
    inv_freq = jnp.exp(idx * step_hi + idx * step_lo)
    ang = pos.astype(F32)[:, None] * inv_freq[None, :]
    cos, sin = jnp.cos(ang), jnp.sin(ang)
    n = pos.shape[0]
    ones = jnp.ones((n, HEAD_DIM - ROPE_DIM), F32)
    zeros8 = jnp.zeros((n, half), F32)
    zeros48 = jnp.zeros((n, HEAD_DIM - ROPE_DIM), F32)
    c = jnp.concatenate([cos, cos, ones], axis=1)
    s_up = jnp.concatenate([zeros8, sin, zeros48], axis=1)
    s_dn = jnp.concatenate([-sin, zeros8, zeros48], axis=1)
    tile2 = lambda t: jnp.concatenate([t, t], axis=1)
    return tile2(c), tile2(s_up), tile2(s_dn)


def _inproj_kernel(x_ref, gain_ref, w_ref, cos_ref, sup_ref, sdn_ref, *out_refs, prompt, tiles_per_seq):
    if prompt:
        qa_ref, ka_ref, k_ref, v_ref, vb_ref, km_ref, u_ref, pz_ref, g_ref = out_refs
    else:
        qa_ref, k_ref, v_ref, u_ref, pz_ref, g_ref = out_refs
    tm = x_ref.shape[0]
    h = _rms(x_ref[...], gain_ref[...]).astype(BF16)
    cosv, sup, sdn = cos_ref[...], sup_ref[...], sdn_ref[...]

    def rope(slab):
        return slab * cosv + pltpu.roll(slab, ROPE_DIM // 2, 1) * sup + pltpu.roll(slab, LANES - ROPE_DIM // 2, 1) * sdn

    q_all = _dot(h, w_ref[:, _C_Q:_C_K])
    for hd in range(N_HEADS):
        qa_ref[:, hd * SLAB:(hd + 1) * SLAB] = rope(q_all[:, hd * SLAB:(hd + 1) * SLAB]).astype(BF16)

    k_all = _dot(h, w_ref[:, _C_K:_C_V])
    lane = lax.broadcasted_iota(jnp.int32, (tm, LANES), 1)
    if prompt:
        row = lax.broadcasted_iota(jnp.int32, (tm, LANES), 0)
        blk = ((pl.program_id(0) % tiles_per_seq) * tm + row) // MOBA_BLOCK
        onehot_even = (lane == blk + HEAD_DIM).astype(F32)
        onehot_odd = (lane == blk).astype(F32)
    for pair in range(N_HEADS // 2):
        k_even = rope(k_all[:, (2 * pair) * SLAB:(2 * pair + 1) * SLAB])
        k_odd = rope(k_all[:, (2 * pair + 1) * SLAB:(2 * pair + 2) * SLAB])
        k_ref[:, pair * LANES:(pair + 1) * LANES] = jnp.where(lane < HEAD_DIM, k_even, k_odd)
        if prompt:
            for hd, k_slab, onehot in ((2 * pair, k_even, onehot_even), (2 * pair + 1, k_odd, onehot_odd)):
                ka_ref[:, hd * SLAB:(hd + 1) * SLAB] = (k_slab + onehot).astype(BF16)
                km = jnp.mean(k_slab.reshape(tm // MOBA_BLOCK, MOBA_BLOCK, SLAB), axis=1)
                km_ref[:, 0, hd * SLAB:(hd + 1) * SLAB] = km

    v_all = _dot(h, w_ref[:, _C_V:_C_GLU])
    v_ref[...] = v_all
    if prompt:
        vb_ref[...] = v_all.astype(BF16)

    glu = _dot(h, w_ref[:, _C_GLU:_C_PZ])
    u_ref[...] = glu[:, :D_CONV] * jax.nn.sigmoid(glu[:, D_CONV:])
    pz_ref[...] = _dot(h, w_ref[:, _C_PZ:_C_G])
    g_ref[...] = jax.nn.sigmoid(_dot(h, w_ref[:, _C_G:]))


def _inproj(x2d, gain, w_aug, tables, *, prompt, seq_len, tm):
    t, d_model = x2d.shape
    n_tiles = t // tm
    cos_t, sup_t, sdn_t = tables
    n_tab = cos_t.shape[0] // tm
    tiles_per_seq = max(seq_len // tm, 1)
    n_cols = w_aug.shape[1]
    row = lambda c: pl.BlockSpec((tm, c), lambda i: (i, 0))
    tab = pl.BlockSpec((tm, LANES), lambda i: (i % n_tab, 0))
    f = jax.ShapeDtypeStruct
    outs = [(f((t, N_HEADS * SLAB), BF16), row(N_HEADS * SLAB))]
    if prompt:
        outs.append((f((t, N_HEADS * SLAB), BF16), row(N_HEADS * SLAB)))
    outs += [(f((t, D_ATTN), F32), row(D_ATTN)), (f((t, D_ATTN), F32), row(D_ATTN))]
    if prompt:
        bpt = tm // MOBA_BLOCK
        outs.append((f((t, D_ATTN), BF16), row(D_ATTN)))
        outs.append((f((t // MOBA_BLOCK, 1, N_HEADS * SLAB), F32),
                     pl.BlockSpec((bpt, 1, N_HEADS * SLAB), lambda i: (i, 0, 0))))
    outs += [(f((t, D_CONV), F32), row(D_CONV)), (f((t, D_POOL), F32), row(D_POOL)),
             (f((t, N_BRANCHES * d_model), F32), row(N_BRANCHES * d_model))]
    return pl.pallas_call(
        functools.partial(_inproj_kernel, prompt=prompt, tiles_per_seq=tiles_per_seq),
        grid=(n_tiles,),
        in_specs=[row(d_model), _resident((1, d_model), lambda i: (0, 0)),
                  _resident((d_model, n_cols), lambda i: (0, 0)), tab, tab, tab],
        out_specs=[o[1] for o in outs],
        out_shape=[o[0] for o in outs],
        compiler_params=_cparams(("parallel",)),
    )(x2d, gain.reshape(1, d_model), w_aug, cos_t, sup_t, sdn_t)


def _top_blocks(gate, valid, idx, axis):
    g = jnp.where(valid, gate, -jnp.inf)
    picked = jnp.zeros(gate.shape, jnp.bool_)
    for _ in range(MOBA_TOPK):
        m = jnp.max(g, axis=axis, keepdims=True)
        first = jnp.min(jnp.where(g == m, idx, jnp.int32(1 << 20)), axis=axis, keepdims=True)
        pick = (idx == first) & (m > -jnp.inf)
        picked = picked | pick
        g = jnp.where(pick, -jnp.inf, g)
    return picked


def _attn_prompt_kernel(q_ref, k_ref, v_ref, km_ref, o_ref):
    ti = pl.program_id(1)
    tq = q_ref.shape[0]
    nb = km_ref.shape[0]
    span = 2 * MOBA_BLOCK
    lane = lax.broadcasted_iota(jnp.int32, (tq, LANES), 1)
    blk_row = lax.broadcasted_iota(jnp.int32, (nb, tq), 0)
    own_t = ti * (tq // MOBA_BLOCK) + lax.broadcasted_iota(jnp.int32, (nb, tq), 1) // MOBA_BLOCK
    row = lax.broadcasted_iota(jnp.int32, (tq, span), 0)
    col = lax.broadcasted_iota(jnp.int32, (tq, span), 1)

    q_full = []
    for hd in range(N_HEADS):
        qs = q_ref[:, hd * SLAB:(hd + 1) * SLAB]
        gate_t = _dot_t(km_ref[:, hd * SLAB:(hd + 1) * SLAB].astype(BF16), qs)
        picked = _top_blocks(gate_t, blk_row < own_t, blk_row, 0)
        pen_t = jnp.where(picked | (blk_row == own_t), 0.0, NEG)
        pieces = [pen_t]
        if nb < HEAD_DIM:
            pieces.append(jnp.full((HEAD_DIM - nb, tq), NEG, F32))
        zeros = jnp.zeros((HEAD_DIM, tq), F32)
        pieces = [zeros] + pieces if hd % 2 == 0 else pieces + [zeros]
        pen = jnp.concatenate(pieces, axis=0).T
        q_full.append((qs.astype(F32) + pen).astype(BF16))

    def sweep(start, carry, diag):
        start = pl.multiple_of(start, span)
        new = []
        for pair in range(N_HEADS // 2):
            vb = v_ref[pl.ds(start, span), pair * LANES:(pair + 1) * LANES]
            upd = []
            for hd in (2 * pair, 2 * pair + 1):
                s = _dot_t(q_full[hd], k_ref[pl.ds(start, span), hd * SLAB:(hd + 1) * SLAB])
                if diag is not None:
                    s = jnp.where(col + diag * span <= row, s, NEG)
                s_max = jnp.max(s, axis=1, keepdims=True)
                if carry is None:
                    m_new, alpha = s_max, None
                else:
                    m_old, l_old = carry[3 * pair + (hd % 2)], carry[3 * pair + 2][hd % 2]
                    m_new = jnp.maximum(m_old, s_max)
                    alpha = jnp.exp(m_old - m_new)
                p = jnp.exp(s - m_new)
                l_new = jnp.sum(p, axis=1, keepdims=True)
                if alpha is not None:
                    l_new = alpha * l_old + l_new
                upd.append((m_new, alpha, l_new, p.astype(BF16)))
            pv2 = _dot(jnp.concatenate([upd[0][3], upd[1][3]], axis=0), vb)
            pv = jnp.where(lane < HEAD_DIM, pv2[:tq], pv2[tq:])
            if carry is None:
                acc = pv
            else:
                acc = jnp.where(lane < HEAD_DIM, upd[0][1], upd[1][1]) * carry[3 * pair + 2][2] + pv
            new += [upd[0][0], upd[1][0], (upd[0][2], upd[1][2], acc)]
        return tuple(new)

    state = None
    for d in range(tq // span):
        state = sweep(ti * tq + d * span, state, d)
    state = lax.fori_loop(0, ti * (tq // span), lambda i, c: sweep(i * span, c, None), state)
    for pair in range(N_HEADS // 2):
        l0, l1, acc = state[3 * pair + 2]
        o_ref[:, pair * LANES:(pair + 1) * LANES] = (acc / jnp.where(lane < HEAD_DIM, l0, l1)).astype(o_ref.dtype)


def _attn_prompt(q_aug, k_aug, v_bf, kmean, *, n_seq, seq_len, tq):
    t = q_aug.shape[0]
    nb = seq_len // MOBA_BLOCK
    n_tiles = seq_len // tq
    assert tq % (2 * MOBA_BLOCK) == 0 and seq_len % tq == 0
    km = kmean.reshape(n_seq, nb, N_HEADS * SLAB)
    return pl.pallas_call(
        _attn_prompt_kernel,
        grid=(n_seq, n_tiles),
        in_specs=[
            pl.BlockSpec((tq, N_HEADS * SLAB), lambda b, i: (b * n_tiles + i, 0)),
            _resident((seq_len, N_HEADS * SLAB), lambda b, i: (b, 0)),
            _resident((seq_len, D_ATTN), lambda b, i: (b, 0)),
            pl.BlockSpec((None, nb, N_HEADS * SLAB), lambda b, i: (b, 0, 0)),
        ],
        out_specs=pl.BlockSpec((tq, D_ATTN), lambda b, i: (b * n_tiles + i, 0)),
        out_shape=jax.ShapeDtypeStruct((t, D_ATTN), BF16),
        compiler_params=_cparams(("parallel", "arbitrary")),
    )(q_aug, k_aug, v_bf, km)


def _attn_sample_kernel(pt_ref, q_ref, kn_ref, vn_ref, e_ref, *rest, pages_per_step, n_steps, page_size):
    del pt_ref
    k_refs = rest[:pages_per_step]
    v_refs = rest[pages_per_step:2 * pages_per_step]
    o_ref = rest[2 * pages_per_step]
    qm_scr, s_scr, p_scr, gate_scr, acc_scr, l_scr = rest[2 * pages_per_step + 1:]
    j = pl.program_id(1)
    ls = q_ref.shape[0]
    rows = N_HEADS * ls
    pages_per_block = MOBA_BLOCK // page_size
    lane_d = lax.broadcasted_iota(jnp.int32, (ls, D_ATTN), 1)
    lane_r = lax.broadcasted_iota(jnp.int32, (rows, LANES), 1)

    @pl.when(j == 0)
    def _():
        qa = q_ref[...]
        lane = lax.broadcasted_iota(jnp.int32, (ls, LANES), 1)
        q_nat = jnp.concatenate(
            [jnp.where(lane < HEAD_DIM, qa[:, (2 * p) * SLAB:(2 * p + 1) * SLAB],
                       qa[:, (2 * p + 1) * SLAB:(2 * p + 2) * SLAB]) for p in range(N_HEADS // 2)], axis=1)
        qm_scr[...] = jnp.concatenate(
            [jnp.where(lane_d // HEAD_DIM == hd, q_nat, jnp.zeros_like(q_nat)) for hd in range(N_HEADS)], axis=0)
        gate_scr[...] = jnp.zeros_like(gate_scr)

    @pl.when(j < n_steps)
    def _():
        qm = qm_scr[...]
        gate = gate_scr[...]
        for i in range(pages_per_step):
            page = j * pages_per_step + i
            col = pl.multiple_of(page * page_size, page_size)
            s = _dot(qm, k_refs[i][...].astype(BF16))
            s_scr[:, pl.ds(col, page_size)] = s
            gate = gate + jnp.where(lane_r == page // pages_per_block, jnp.sum(s, axis=1, keepdims=True), 0.0)
        gate_scr[...] = gate

    @pl.when(j == n_steps - 1)
    def _():
        qm = qm_scr[...]
        n_past = n_steps * pages_per_step // pages_per_block
        lane = lane_r
        gate = gate_scr[...] * (1.0 / MOBA_BLOCK)
        picked = _top_blocks(gate, lane < n_past, lane, 1)
        sel = _dot(jnp.where(picked, 1.0, 0.0).astype(BF16), e_ref[...])
        s = jnp.where(sel > 0.5, s_scr[...], NEG)
        pad = jnp.zeros((LANES - ls, D_ATTN), BF16)
        s_own = _dot_t(qm, jnp.concatenate([kn_ref[...].astype(BF16), pad], axis=0))
        qi = lax.broadcasted_iota(jnp.int32, (rows, LANES), 0) % ls
        s_own = jnp.where(lane <= qi, s_own, NEG)
        m = jnp.maximum(jnp.max(s, axis=1, keepdims=True), jnp.max(s_own, axis=1, keepdims=True))
        p = jnp.exp(s - m)
        p_own = jnp.exp(s_own - m)
        l_scr[...] = jnp.sum(p, axis=1, keepdims=True) + jnp.sum(p_own, axis=1, keepdims=True)
        p_scr[...] = p.astype(BF16)
        acc_scr[...] = _dot(p_own.astype(BF16), jnp.concatenate([vn_ref[...].astype(BF16), pad], axis=0))

    @pl.when(j >= n_steps)
    def _():
        acc = acc_scr[...]
        for i in range(pages_per_step):
            col = pl.multiple_of(((j - n_steps) * pages_per_step + i) * page_size, page_size)
            acc = acc + _dot_t(p_scr[:, pl.ds(col, page_size)], v_refs[i][...].astype(BF16))
        acc_scr[...] = acc

    @pl.when(j == 2 * n_steps - 1)
    def _():
        o = acc_scr[...] / l_scr[...]
        out = jnp.zeros((ls, D_ATTN), F32)
        for hd in range(N_HEADS):
            out = out + jnp.where(lane_d // HEAD_DIM == hd, o[hd * ls:(hd + 1) * ls, :], 0.0)
        o_ref[...] = out.astype(o_ref.dtype)


def _attn_sample(q_aug, k_new, v_new, cache_kt, cache_vt, page_table, layer, *, n_seq, seq_len):
    n_pages = page_table.shape[1]
    page_size = cache_kt.shape[3]
    past_len = n_pages * page_size
    assert past_len % MOBA_BLOCK == 0 and MOBA_BLOCK % page_size == 0 and seq_len % 8 == 0
    assert seq_len <= MOBA_BLOCK and past_len // MOBA_BLOCK <= LANES
    pages_per_step = _pick_tile(n_pages, (16, 8, MOBA_BLOCK // page_size))
    n_steps = n_pages // pages_per_step
    rows = N_HEADS * seq_len
    expand = (jnp.arange(past_len)[None, :] // MOBA_BLOCK == jnp.arange(LANES)[:, None]).astype(BF16)

    def k_map(i):
        return lambda b, j, pt: (layer, pt[b, jnp.minimum(j, n_steps - 1) * pages_per_step + i], 0, 0)

    def v_map(i):
        return lambda b, j, pt: (layer, pt[b, jnp.maximum(j - n_steps, 0) * pages_per_step + i], 0, 0)

    page = lambda m: pl.BlockSpec((None, None, D_ATTN, page_size), m)
    grid_spec = pltpu.PrefetchScalarGridSpec(
        num_scalar_prefetch=1,
        grid=(n_seq, 2 * n_steps),
        in_specs=[pl.BlockSpec((seq_len, N_HEADS * SLAB), lambda b, j, pt: (b, 0)),
                  pl.BlockSpec((seq_len, D_ATTN), lambda b, j, pt: (b, 0)),
                  pl.BlockSpec((seq_len, D_ATTN), lambda b, j, pt: (b, 0)),
                  _resident((LANES, past_len), lambda b, j, pt: (0, 0))]
                 + [page(k_map(i)) for i in range(pages_per_step)]
                 + [page(v_map(i)) for i in range(pages_per_step)],
        out_specs=pl.BlockSpec((seq_len, D_ATTN), lambda b, j, pt: (b, 0)),
        scratch_shapes=[pltpu.VMEM((rows, D_ATTN), BF16), pltpu.VMEM((rows, past_len), F32),
                        pltpu.VMEM((rows, past_len), BF16), pltpu.VMEM((rows, LANES), F32),
                        pltpu.VMEM((rows, D_ATTN), F32), pltpu.VMEM((rows, 1), F32)],
    )
    return pl.pallas_call(
        functools.partial(_attn_sample_kernel, pages_per_step=pages_per_step, n_steps=n_steps, page_size=page_size),
        grid_spec=grid_spec,
        out_shape=jax.ShapeDtypeStruct((n_seq * seq_len, D_ATTN), BF16),
        compiler_params=_cparams(("parallel", "arbitrary")),
    )(page_table, q_aug, k_new, v_new, expand, *([cache_kt] * pages_per_step), *([cache_vt] * pages_per_step))


def _transposed_pages(cache):
    depth, n_pool, page_size = cache.shape[:3]
    return jnp.transpose(cache, (0, 1, 3, 4, 2)).reshape(depth, n_pool, D_ATTN, page_size)


def _mix_kernel(x_ref, u_ref, uh_ref, z_ref, zh_ref, a_ref, g_ref,
                cw_ref, cb_ref, lg_ref, lb_ref, wc_ref, pl_ref, ps_ref, wp_ref, wa_ref, wo_ref,
                o_ref, ext_ref, zext_ref, *, prompt, n_pool_prev):
    tl = u_ref.shape[0]
    i = pl.program_id(1)
    d_model = x_ref.shape[1]

    uh, zh = uh_ref[...], zh_ref[...]
    if prompt:
        uh = jnp.where(i == 0, 0.0, uh)
        zh = jnp.where(i == 0, 0.0, zh)
    ext_ref[0:CONV_HALO, :] = uh
    ext_ref[CONV_HALO:, :] = u_ref[...]
    zext_ref[0:POOL_HALO, :] = zh
    z = z_ref[...]
    zext_ref[POOL_HALO:, :] = z

    y = jnp.zeros((tl, D_CONV), F32) + cb_ref[...]
    base = CONV_HALO - (CONV_WIDTH - 1)
    for j in range(CONV_WIDTH):
        y = y + ext_ref[base + j:base + j + tl, :] * cw_ref[j:j + 1, :]
    mu = jnp.mean(y, axis=-1, keepdims=True)
    yc = y - mu
    var = jnp.mean(yc * yc, axis=-1, keepdims=True)
    yn = yc * lax.rsqrt(var + EPS) * lg_ref[...] + lb_ref[...]
    c = _dot(jax.nn.silu(yn).astype(BF16), wc_ref[...])

    pos = i * tl + lax.broadcasted_iota(jnp.int32, (tl, 1), 0)
    parts = []
    for gi, w in enumerate(POOL_WINDOWS):
        lo, hi = gi * POOL_GROUP, (gi + 1) * POOL_GROUP
        tot = z[:, lo:hi]
        for back in range(1, w):
            tot = tot + zext_ref[POOL_HALO - back:POOL_HALO - back + tl, lo:hi]
        cnt = jnp.minimum(w, pos + 1 + n_pool_prev).astype(F32)
        d = tot / cnt - z[:, lo:hi]
        parts.append(_dot(d.astype(BF16), pl_ref[gi]))
    pm = jnp.concatenate(parts, axis=1) * ps_ref[...]
    p = _dot(pm.astype(BF16), wp_ref[...])

    a = _dot(a_ref[...], wa_ref[...])
    merged = (g_ref[:, 0:d_model] * a + g_ref[:, d_model:2 * d_model] * c
              + g_ref[:, 2 * d_model:3 * d_model] * p)
    o_ref[...] = x_ref[...] + _dot(merged.astype(BF16), wo_ref[...])


def _mix(x, u, z, attn, gates, conv_hist, pool_hist, lw, *, prompt, tl):
    n_seq, seq_len, d_model = x.shape
    n_tiles = seq_len // tl
    seq = lambda c: pl.BlockSpec((None, tl, c), lambda b, i: (b, i, 0))
    if prompt:
        uh_arr, zh_arr = u, z
        uh_spec = pl.BlockSpec((None, CONV_HALO, D_CONV),
                               lambda b, i: (b, jnp.maximum(i * (tl // CONV_HALO) - 1, 0), 0))
        zh_spec = pl.BlockSpec((None, POOL_HALO, D_POOL),
                               lambda b, i: (b, jnp.maximum(i * (tl // POOL_HALO) - 1, 0), 0))
        n_pool_prev = 0
    else:
        assert n_tiles == 1
        uh_arr = jnp.pad(conv_hist, ((0, 0), (CONV_HALO - conv_hist.shape[1], 0), (0, 0)))
        zh_arr = jnp.pad(pool_hist, ((0, 0), (POOL_HALO - pool_hist.shape[1], 0), (0, 0)))
        uh_spec = pl.BlockSpec((None, CONV_HALO, D_CONV), lambda b, i: (b, 0, 0))
        zh_spec = pl.BlockSpec((None, POOL_HALO, D_POOL), lambda b, i: (b, 0, 0))
        n_pool_prev = POOL_STATE
    const = lambda shape: _resident(shape, lambda b, i: (0,) * len(shape))
    return pl.pallas_call(
        functools.partial(_mix_kernel, prompt=prompt, n_pool_prev=n_pool_prev),
        grid=(n_seq, n_tiles),
        in_specs=[seq(d_model), seq(D_CONV), uh_spec, seq(D_POOL), zh_spec, seq(D_ATTN), seq(N_BRANCHES * d_model),
                  const((CONV_WIDTH, D_CONV)), const((1, D_CONV)), const((1, D_CONV)), const((1, D_CONV)),
                  const((D_CONV, d_model)), const((len(POOL_WINDOWS), POOL_GROUP, POOL_GROUP)),
                  const((1, D_POOL)), const((D_POOL, d_model)), const((D_ATTN, d_model)),
                  const((d_model, d_model))],
        out_specs=seq(d_model),
        out_shape=jax.ShapeDtypeStruct(x.shape, F32),
        scratch_shapes=[pltpu.VMEM((CONV_HALO + tl, D_CONV), F32), pltpu.VMEM((POOL_HALO + tl, D_POOL), F32)],
        compiler_params=_cparams(("parallel", "arbitrary")),
    )(x, u, uh_arr, z, zh_arr, attn, gates,
      lw["conv_w"], lw["conv_b"], lw["conv_ln_g"], lw["conv_ln_b"], lw["w_conv_out"], lw["pool_lin"],
      lw["pool_scale"], lw["w_pool_out"], lw["w_attn_out"], lw["w_out"])


def _ffn_chunk(d_ff):
    for n in (2, 4, 1):
        if d_ff % (n * LANES) == 0 and d_ff // n <= 2048:
            return d_ff // n
    return d_ff


def _ffn_kernel(x_ref, gain_ref, w1_ref, w3_ref, w2_ref, fg_ref, o_ref, *, chunk, final_norm):
    x = x_ref[...]
    h = _rms(x, gain_ref[...]).astype(BF16)
    acc = x
    for c0 in range(0, w1_ref.shape[1], chunk):
        act = jax.nn.silu(_dot(h, w1_ref[:, c0:c0 + chunk])) * _dot(h, w3_ref[:, c0:c0 + chunk])
        acc = acc + _dot(act.astype(BF16), w2_ref[c0:c0 + chunk, :])
    o_ref[...] = _rms(acc, fg_ref[...]) if final_norm else acc


def _ffn(x2d, gain, w1, w3, w2, final_gain, *, tm):
    t, d_model = x2d.shape
    d_ff = w1.shape[1]
    final_norm = final_gain is not None
    fg = (final_gain if final_norm else gain).reshape(1, d_model)
    row = pl.BlockSpec((tm, d_model), lambda i: (i, 0))
    vec = _resident((1, d_model), lambda i: (0, 0))
    return pl.pallas_call(
        functools.partial(_ffn_kernel, chunk=_ffn_chunk(d_ff), final_norm=final_norm),
        grid=(t // tm,),
        in_specs=[row, vec, _resident((d_model, d_ff), lambda i: (0, 0)), _resident((d_model, d_ff), lambda i: (0, 0)),
                  _resident((d_ff, d_model), lambda i: (0, 0)), vec],
        out_specs=row,
        out_shape=jax.ShapeDtypeStruct((t, d_model), F32),
        compiler_params=_cparams(("parallel",)),
    )(x2d, gain.reshape(1, d_model), w1, w3, w2, fg)


MOE_CHUNK = 256


def _dot_halves(a_ref, b):
    half = a_ref.shape[0] // 2
    return jnp.concatenate([_dot(a_ref[:half, :], b), _dot(a_ref[half:, :], b)], axis=0)


def _router_kernel(x_ref, gain_ref, r_ref, tri_ref, rank_ref, rankt_ref, combt_ref, cnt_ref):
    tm = x_ref.shape[0]
    lane = lax.broadcasted_iota(jnp.int32, (tm, LANES), 1)
    h32 = _rms(x_ref[...], gain_ref[...])
    logits = jnp.dot(h32, r_ref[...], preferred_element_type=F32, precision=lax.Precision.HIGHEST)
    logits = jnp.where(lane < N_EXPERTS, logits, -jnp.inf)
    m1 = jnp.max(logits, axis=1, keepdims=True)
    i1 = jnp.min(jnp.where(logits == m1, lane, LANES), axis=1, keepdims=True)
    rest = jnp.where(lane == i1, -jnp.inf, logits)
    m2 = jnp.max(rest, axis=1, keepdims=True)
    i2 = jnp.min(jnp.where(rest == m2, lane, LANES), axis=1, keepdims=True)
    e2 = jnp.exp(m2 - m1)
    den = 1.0 + e2
    comb = jnp.where(lane == i1, 1.0 / den, 0.0) + jnp.where(lane == i2, e2 / den, 0.0)
    routed = (lane == i1) | (lane == i2)
    ones = jnp.where(routed, 1.0, 0.0)
    before = _dot(tri_ref[...], ones.astype(BF16))
    rank = jnp.where(routed, before, -1.0)
    rank_ref[...] = rank
    rankt_ref[...] = rank.T[:N_EXPERTS]
    combt_ref[...] = comb.T[:N_EXPERTS]
    cnt_ref[...] = jnp.sum(ones, axis=0, keepdims=True).astype(jnp.int32)


def _moe_kernel(cnt_ref, x_ref, gain_ref, rank_ref, rankt_ref, combt_ref, w1t_ref, w3t_ref, w2t_ref, fg_ref,
                o_ref, ht_scr, acct_scr, yt_scr, *, final_norm):
    i, e, f = pl.program_id(0), pl.program_id(1), pl.program_id(2)
    n_e, n_f = pl.num_programs(1), pl.num_programs(2)
    tm = x_ref.shape[0]
    n_chunks = (cnt_ref[i * LANES + e] + MOE_CHUNK - 1) // MOE_CHUNK

    @pl.when((e == 0) & (f == 0))
    def _():
        ht_scr[...] = _rms(x_ref[...], gain_ref[...]).T.astype(BF16)
        yt_scr[...] = jnp.zeros_like(yt_scr)

    lane = lax.broadcasted_iota(jnp.int32, (tm, LANES), 1)
    rank_col = jnp.sum(jnp.where(lane == e, rank_ref[...], 0.0), axis=1, keepdims=True)
    slot_col = lax.broadcasted_iota(jnp.int32, (tm, MOE_CHUNK), 1).astype(F32)

    def expert_pass(c, carry):
        cols = pl.ds(pl.multiple_of(c * MOE_CHUNK, MOE_CHUNK), MOE_CHUNK)
        take = jnp.where(rank_col - (c * MOE_CHUNK).astype(F32) == slot_col, 1.0, 0.0).astype(BF16)
        xt = _dot_halves(ht_scr, take).astype(BF16)
        act = jax.nn.silu(_dot(w1t_ref[...], xt)) * _dot(w3t_ref[...], xt)
        part = _dot_halves(w2t_ref, act.astype(BF16))

        @pl.when(f == 0)
        def _():
            acct_scr[:, cols] = part

        @pl.when(f > 0)
        def _():
            acct_scr[:, cols] += part

        return carry

    lax.fori_loop(0, n_chunks, expert_pass, 0)

    @pl.when(f == n_f - 1)
    def _():
        rank_row = rankt_ref[pl.ds(e, 1), :]
        comb_row = combt_ref[pl.ds(e, 1), :]
        slot_row = lax.broadcasted_iota(jnp.int32, (MOE_CHUNK, tm), 0).astype(F32)

        def scatter_pass(c, carry):
            cols = pl.ds(pl.multiple_of(c * MOE_CHUNK, MOE_CHUNK), MOE_CHUNK)
            put = jnp.where(rank_row - (c * MOE_CHUNK).astype(F32) == slot_row, 1.0, 0.0).astype(BF16)
            a = acct_scr[:, cols]
            hi = a.astype(BF16)
            lo = (a - hi.astype(F32)).astype(BF16)
            yt_scr[...] += comb_row * (_dot(hi, put) + _dot(lo, put))
            return carry

        lax.fori_loop(0, n_chunks, scatter_pass, 0)

    @pl.when((e == n_e - 1) & (f == n_f - 1))
    def _():
        out = x_ref[...] + yt_scr[...].T
        o_ref[...] = _rms(out, fg_ref[...]) if final_norm else out


def _moe(x2d, gain, router, w1t, w3t, w2t, final_gain, *, tm):
    t, d_model = x2d.shape
    n_e, d_ff, _ = w1t.shape
    assert n_e == N_EXPERTS and tm % MOE_CHUNK == 0
    n_tiles = t // tm
    chunk = _ffn_chunk(d_ff)
    n_f = d_ff // chunk
    final_norm = final_gain is not None
    fg = (final_gain if final_norm else gain).reshape(1, d_model)
    gain2 = gain.reshape(1, d_model)
    r_pad = jnp.pad(router, ((0, 0), (0, LANES - n_e)))
    earlier = (jnp.arange(tm)[None, :] < jnp.arange(tm)[:, None]).astype(BF16)

    rank, rank_t, comb_t, counts = pl.pallas_call(
        _router_kernel,
        grid=(n_tiles,),
        in_specs=[pl.BlockSpec((tm, d_model), lambda i: (i, 0)), _resident((1, d_model), lambda i: (0, 0)),
                  _resident((d_model, LANES), lambda i: (0, 0)), _resident((tm, tm), lambda i: (0, 0))],
        out_specs=[pl.BlockSpec((tm, LANES), lambda i: (i, 0)),
                   pl.BlockSpec((None, N_EXPERTS, tm), lambda i: (i, 0, 0)),
                   pl.BlockSpec((None, N_EXPERTS, tm), lambda i: (i, 0, 0)),
                   pl.BlockSpec((None, 1, LANES), lambda i: (i, 0, 0))],
        out_shape=[jax.ShapeDtypeStruct((t, LANES), F32), jax.ShapeDtypeStruct((n_tiles, N_EXPERTS, tm), F32),
                   jax.ShapeDtypeStruct((n_tiles, N_EXPERTS, tm), F32),
                   jax.ShapeDtypeStruct((n_tiles, 1, LANES), jnp.int32)],
        compiler_params=_cparams(("parallel",)),
    )(x2d, gain2, r_pad, earlier)

    row = pl.BlockSpec((tm, d_model), lambda i, e, f, cnt: (i, 0))
    vec = _resident((1, d_model), lambda i, e, f, cnt: (0, 0))
    tile_t = pl.BlockSpec((None, N_EXPERTS, tm), lambda i, e, f, cnt: (i, 0, 0))
    grid_spec = pltpu.PrefetchScalarGridSpec(
        num_scalar_prefetch=1,
        grid=(n_tiles, n_e, n_f),
        in_specs=[row, vec, pl.BlockSpec((tm, LANES), lambda i, e, f, cnt: (i, 0)), tile_t, tile_t,
                  pl.BlockSpec((None, chunk, d_model), lambda i, e, f, cnt: (e, f, 0)),
                  pl.BlockSpec((None, chunk, d_model), lambda i, e, f, cnt: (e, f, 0)),
                  pl.BlockSpec((None, d_model, chunk), lambda i, e, f, cnt: (e, 0, f)), vec],
        out_specs=row,
        scratch_shapes=[pltpu.VMEM((d_model, tm), BF16), pltpu.VMEM((d_model, tm), F32),
                        pltpu.VMEM((d_model, tm), F32)],
    )
    return pl.pallas_call(
        functools.partial(_moe_kernel, final_norm=final_norm),
        grid_spec=grid_spec,
        out_shape=jax.ShapeDtypeStruct((t, d_model), F32),
        compiler_params=_cparams(("parallel", "arbitrary", "arbitrary")),
    )(counts.reshape(n_tiles * LANES), x2d, gain2, rank, rank_t, comb_t, w1t, w3t, w2t, fg)


def _pick_tile(n, prefs):
    for p in prefs:
        if n % p == 0:
            return p
    return n


def _forward(x, pos, cache, layers, norm_final, *, prompt):
    n_seq, seq_len, d_model = x.shape
    t = n_seq * seq_len
    depth = len(layers)
    if prompt:
        assert seq_len % MOBA_BLOCK == 0 and seq_len // MOBA_BLOCK <= HEAD_DIM
        tm_in = _pick_tile(seq_len, (512, MOBA_BLOCK))
        tables = _rope_tables(pos)
        tl = _pick_tile(seq_len, (512, 256, 128, 64, 32))
        tm_ffn = _pick_tile(t, (512, 256))
        tm_moe = _pick_tile(t, (1024, 512, 256))
    else:
        tm_in = t
        tables = tuple(jnp.tile(tb, (n_seq, 1)) for tb in _rope_tables(pos))
        tl = seq_len
        tm_ffn = tm_moe = t
        cache_kt, cache_vt, page_table, state_conv, state_pool = cache
    ks, vs, cs, ps = [], [], [], []
    for l, lw in enumerate(layers):
        outs = _inproj(x.reshape(t, d_model), lw["norm_mix"], lw["w_in_aug"], tables,
                       prompt=prompt, seq_len=seq_len, tm=tm_in)
        if prompt:
            q_aug, k_aug, k_new, v_new, v_bf, kmean, u, z, gates = outs
            attn = _attn_prompt(q_aug, k_aug, v_bf, kmean, n_seq=n_seq, seq_len=seq_len, tq=2 * MOBA_BLOCK)
            conv_hist = pool_hist = None
        else:
            q_aug, k_new, v_new, u, z, gates = outs
            attn = _attn_sample(q_aug, k_new, v_new, cache_kt, cache_vt, page_table, l,
                                n_seq=n_seq, seq_len=seq_len)
            conv_hist, pool_hist = state_conv[l], state_pool[l]
        r3 = lambda a: a.reshape(n_seq, seq_len, a.shape[-1])
        u3, z3 = r3(u), r3(z)
        x = _mix(x, u3, z3, r3(attn), r3(gates), conv_hist, pool_hist, lw, prompt=prompt, tl=tl)
        final_gain = norm_final if l == depth - 1 else None
        x2d = x.reshape(t, d_model)
        if lw["kind"] == "dense":
            x2d = _ffn(x2d, lw["norm_ffn"], lw["w1"], lw["w3"], lw["w2"], final_gain, tm=tm_ffn)
        else:
            x2d = _moe(x2d, lw["norm_ffn"], lw["router"], lw["w1"], lw["w3"], lw["w2"], final_gain, tm=tm_moe)
        x = x2d.reshape(n_seq, seq_len, d_model)
        ks.append(k_new.reshape(n_seq, seq_len, N_HEADS, HEAD_DIM))
        vs.append(v_new.reshape(n_seq, seq_len, N_HEADS, HEAD_DIM))
        if prompt:
            cs.append(u3[:, seq_len - (CONV_WIDTH - 1):])
            ps.append(z3[:, seq_len - POOL_STATE:])
        else:
            cs.append(jnp.concatenate([conv_hist, u3], axis=1)[:, -(CONV_WIDTH - 1):])
            ps.append(jnp.concatenate([pool_hist, z3], axis=1)[:, -POOL_STATE:])
    return x, jnp.stack(ks), jnp.stack(vs), jnp.stack(cs), jnp.stack(ps)


def kernel(x_prompt, x_sample, cache_k, cache_v, page_table, state_conv, state_pool, norm_mix, norm_ffn, norm_final, w_in, w_attn_out, conv_w, conv_b, conv_ln_g, conv_ln_b, w_conv_out, pool_lin, pool_scale, w_pool_out, w_out, ffn_w1, ffn_w3, ffn_w2, moe_router, moe_w1, moe_w3, moe_w2):
    depth = norm_mix.shape[0]
    layers = []
    for l in range(depth):
        lw = dict(
            norm_mix=norm_mix[l], norm_ffn=norm_ffn[l],
            w_in_aug=_prep_w_in(w_in[l]),
            w_attn_out=w_attn_out[l].astype(BF16), conv_w=conv_w[l], conv_b=conv_b[l].reshape(1, -1),
            conv_ln_g=conv_ln_g[l].reshape(1, -1), conv_ln_b=conv_ln_b[l].reshape(1, -1),
            w_conv_out=w_conv_out[l].astype(BF16), pool_lin=pool_lin[l].astype(BF16),
            pool_scale=pool_scale[l].reshape(1, -1), w_pool_out=w_pool_out[l].astype(BF16),
            w_out=w_out[l].astype(BF16))
        i = l // 2
        if l % 2 == 0:
            lw.update(kind="dense", w1=ffn_w1[i].astype(BF16), w3=ffn_w3[i].astype(BF16), w2=ffn_w2[i].astype(BF16))
        else:
            tr = lambda w: jnp.swapaxes(w, 1, 2).astype(BF16)
            lw.update(kind="moe", router=moe_router[i], w1=tr(moe_w1[i]), w3=tr(moe_w3[i]), w2=tr(moe_w2[i]))
        layers.append(lw)

    past_len = page_table.shape[1] * cache_k.shape[2]
    pos_prompt = jnp.arange(x_prompt.shape[1], dtype=jnp.int32)
    pos_sample = past_len + jnp.arange(x_sample.shape[1], dtype=jnp.int32)
    y_p, k_p, v_p, conv_p, pool_p = _forward(x_prompt, pos_prompt, None, layers, norm_final, prompt=True)
    cache = (_transposed_pages(cache_k), _transposed_pages(cache_v), page_table, state_conv, state_pool)
    y_s, k_s, v_s, conv_s, pool_s = _forward(x_sample, pos_sample, cache, layers, norm_final, prompt=False)
    return (y_p, y_s, k_p, v_p, k_s, v_s, conv_p, conv_s, pool_p, pool_s)
```

```python
import functools
import math

import jax
import jax.numpy as jnp
import numpy as np
from jax import lax
from jax.experimental import pallas as pl
from jax.experimental.pallas import tpu as pltpu

F32 = jnp.float32
BF16 = jnp.bfloat16

N_HEADS = 8
HEAD_DIM = 64
D_ATTN = N_HEADS * HEAD_DIM
ROPE_DIM = HEAD_DIM // 4
ROPE_THETA = 500000.0
MOBA_BLOCK = 256
MOBA_TOPK = 3
D_CONV = 512
CONV_WIDTH = 31
D_POOL = 512
POOL_WINDOWS = (2, 4, 8, 16)
POOL_GROUP = D_POOL // len(POOL_WINDOWS)
POOL_STATE = max(POOL_WINDOWS) - 1
N_BRANCHES = 3
N_EXPERTS = 8
EPS = 1e-6

LANES = 128
SLAB = 2 * HEAD_DIM
CONV_HALO = 32
POOL_HALO = 16
NEG = -1e30
VMEM_LIMIT = 56 * 1024 * 1024


def _cparams(sem):
    return pltpu.CompilerParams(dimension_semantics=sem, vmem_limit_bytes=VMEM_LIMIT)


def _resident(shape, index_map):
    return pl.BlockSpec(shape, index_map, pipeline_mode=pl.Buffered(1))


def _rms(x, gain):
    ms = jnp.mean(x * x, axis=-1, keepdims=True)
    return x * lax.rsqrt(ms + EPS) * gain


def _dot(a, b):
    return jnp.dot(a, b, preferred_element_type=F32)


def _dot_t(a, b):
    return lax.dot_general(a, b, (((1,), (1,)), ((), ())), preferred_element_type=F32)


_C_Q = 0
_C_K = _C_Q + N_HEADS * SLAB
_C_V = _C_K + N_HEADS * SLAB
_C_GLU = _C_V + D_ATTN
_C_PZ = _C_GLU + 2 * D_CONV
_C_G = _C_PZ + D_POOL


def _slab_weights(w):
    d = w.shape[0]
    w = w.reshape(d, N_HEADS, HEAD_DIM)
    z = jnp.zeros_like(w)
    even = jnp.concatenate([w, z], axis=-1)
    odd = jnp.concatenate([z, w], axis=-1)
    is_even = (jnp.arange(N_HEADS) % 2 == 0)[None, :, None]
    return jnp.where(is_even, even, odd).reshape(d, N_HEADS * SLAB)


def _prep_w_in(w_in):
    wq = w_in[:, :D_ATTN] * (HEAD_DIM ** -0.5)
    wk = w_in[:, D_ATTN:2 * D_ATTN]
    rest = w_in[:, 2 * D_ATTN:]
    return jnp.concatenate([_slab_weights(wq), _slab_weights(wk), rest], axis=1).astype(BF16)


def _rope_tables(pos):
    half = ROPE_DIM // 2
    step = -2.0 * math.log(ROPE_THETA) / ROPE_DIM
    step_hi = float(np.float32(step))
    step_lo = float(np.float32(step - step_hi))
    idx = jnp.arange(half, dtype=F32)
    inv_freq = jnp.exp(idx * step_hi + idx * step_lo)
    ang = pos.astype(F32)[:, None] * inv_freq[None, :]
    cos, sin = jnp.cos(ang), jnp.sin(ang)
    n = pos.shape[0]
    ones = jnp.ones((n, HEAD_DIM - ROPE_DIM), F32)
    zeros8 = jnp.zeros((n, half), F32)
    zeros48 = jnp.zeros((n, HEAD_DIM - ROPE_DIM), F32)
    c = jnp.concatenate([cos, cos, ones], axis=1)
    s_up = jnp.concatenate([zeros8, sin, zeros48], axis=1)
    s_dn = jnp.concatenate([-sin, zeros8, zeros48], axis=1)
    tile2 = lambda t: jnp.concatenate([t, t], axis=1)
    return tile2(c), tile2(s_up), tile2(s_dn)


def _inproj_kernel(x_ref, gain_ref, w_ref, cos_ref, sup_ref, sdn_ref, *refs, prompt, tiles_per_seq, n_carried):
    out_refs = refs[n_carried:]
    if prompt:
        qa_ref, ka_ref, k_ref, v_ref, vb_ref, km_ref, u_ref, pz_ref, g_ref = out_refs
    else:
        qa_ref, k_ref, v_ref, u_ref, pz_ref, g_ref = out_refs
    tm = x_ref.shape[0]
    h = _rms(x_ref[...], gain_ref[...]).astype(BF16)
    cosv, sup, sdn = cos_ref[...], sup_ref[...], sdn_ref[...]

    def rope(slab):
        return slab * cosv + pltpu.roll(slab, ROPE_DIM // 2, 1) * sup + pltpu.roll(slab, LANES - ROPE_DIM // 2, 1) * sdn

    q_all = _dot(h, w_ref[:, _C_Q:_C_K])
    for hd in range(N_HEADS):
        qa_ref[:, hd * SLAB:(hd + 1) * SLAB] = rope(q_all[:, hd * SLAB:(hd + 1) * SLAB]).astype(BF16)

    k_all = _dot(h, w_ref[:, _C_K:_C_V])
    lane = lax.broadcasted_iota(jnp.int32, (tm, LANES), 1)
    if prompt:
        row = lax.broadcasted_iota(jnp.int32, (tm, LANES), 0)
        blk = ((pl.program_id(0) % tiles_per_seq) * tm + row) // MOBA_BLOCK
        onehot_even = (lane == blk + HEAD_DIM).astype(F32)
        onehot_odd = (lane == blk).astype(F32)
    for pair in range(N_HEADS // 2):
        k_even = rope(k_all[:, (2 * pair) * SLAB:(2 * pair + 1) * SLAB])
        k_odd = rope(k_all[:, (2 * pair + 1) * SLAB:(2 * pair + 2) * SLAB])
        k_nat = jnp.where(lane < HEAD_DIM, k_even, k_odd)
        if not prompt:
            k_ref[:, pair * LANES:(pair + 1) * LANES] = k_nat
        else:
            k_ref[pair * LANES:(pair + 1) * LANES, :] = k_nat.T
            for hd, k_slab, onehot in ((2 * pair, k_even, onehot_even), (2 * pair + 1, k_odd, onehot_odd)):
                ka_ref[:, hd * SLAB:(hd + 1) * SLAB] = (k_slab + onehot).astype(BF16)
                km = jnp.mean(k_slab.reshape(tm // MOBA_BLOCK, MOBA_BLOCK, SLAB), axis=1)
                km_ref[:, 0, hd * SLAB:(hd + 1) * SLAB] = km

    v_all = _dot(h, w_ref[:, _C_V:_C_GLU])
    if prompt:
        vb_ref[...] = v_all.astype(BF16)
        for c in range(D_ATTN // LANES):
            v_ref[c * LANES:(c + 1) * LANES, :] = v_all[:, c * LANES:(c + 1) * LANES].T
    else:
        v_ref[...] = v_all

    glu = _dot(h, w_ref[:, _C_GLU:_C_PZ])
    u_ref[...] = glu[:, :D_CONV] * jax.nn.sigmoid(glu[:, D_CONV:])
    pz_ref[...] = _dot(h, w_ref[:, _C_PZ:_C_G])
    g_ref[...] = jax.nn.sigmoid(_dot(h, w_ref[:, _C_G:]))


def _inproj(x2d, gain, w_aug, tables, *, prompt, seq_len, tm, layer=0, depth=1, kv_carried=None):
    t, d_model = x2d.shape
    n_tiles = t // tm
    cos_t, sup_t, sdn_t = tables
    n_tab = cos_t.shape[0] // tm
    tiles_per_seq = max(seq_len // tm, 1)
    n_cols = w_aug.shape[1]
    row = lambda c: pl.BlockSpec((tm, c), lambda i: (i, 0))
    tab = pl.BlockSpec((tm, LANES), lambda i: (i % n_tab, 0))
    f = jax.ShapeDtypeStruct
    outs = [(f((t, N_HEADS * SLAB), BF16), row(N_HEADS * SLAB))]
    if prompt:
        outs.append((f((t, N_HEADS * SLAB), BF16), row(N_HEADS * SLAB)))
        kv_t = (f((depth, t // seq_len, D_ATTN, seq_len), F32),
                pl.BlockSpec((None, None, D_ATTN, tm), lambda i: (layer, i // tiles_per_seq, 0, i % tiles_per_seq)))
        outs += [kv_t, kv_t]
    else:
        outs += [(f((t, D_ATTN), F32), row(D_ATTN)), (f((t, D_ATTN), F32), row(D_ATTN))]
    if prompt:
        bpt = tm // MOBA_BLOCK
        outs.append((f((t, D_ATTN), BF16), row(D_ATTN)))
        outs.append((f((t // MOBA_BLOCK, 1, N_HEADS * SLAB), F32),
                     pl.BlockSpec((bpt, 1, N_HEADS * SLAB), lambda i: (i, 0, 0))))
    outs += [(f((t, D_CONV), F32), row(D_CONV)), (f((t, D_POOL), F32), row(D_POOL)),
             (f((t, N_BRANCHES * d_model), F32), row(N_BRANCHES * d_model))]
    carried = tuple(kv_carried) if kv_carried is not None else ()
    n_in = 6
    return pl.pallas_call(
        functools.partial(_inproj_kernel, prompt=prompt, tiles_per_seq=tiles_per_seq, n_carried=len(carried)),
        grid=(n_tiles,),
        in_specs=[row(d_model), _resident((1, d_model), lambda i: (0, 0)),
                  _resident((d_model, n_cols), lambda i: (0, 0)), tab, tab, tab]
                 + [pl.BlockSpec(memory_space=pl.ANY)] * len(carried),
        out_specs=[o[1] for o in outs],
        out_shape=[o[0] for o in outs],
        input_output_aliases={n_in + j: 2 + j for j in range(len(carried))},
        compiler_params=_cparams(("parallel",)),
    )(x2d, gain.reshape(1, d_model), w_aug, cos_t, sup_t, sdn_t, *carried)


def _top_blocks(gate, valid, idx, axis):
    g = jnp.where(valid, gate, -jnp.inf)
    picked = jnp.zeros(gate.shape, jnp.bool_)
    for _ in range(MOBA_TOPK):
        m = jnp.max(g, axis=axis, keepdims=True)
        first = jnp.min(jnp.where(g == m, idx, jnp.int32(1 << 20)), axis=axis, keepdims=True)
        pick = (idx == first) & (m > -jnp.inf)
        picked = picked | pick
        g = jnp.where(pick, -jnp.inf, g)
    return picked


def _attn_prompt_kernel(q_ref, k_ref, v_ref, km_ref, o_ref):
    ti = pl.program_id(1)
    tq = q_ref.shape[0]
    nb = km_ref.shape[0]
    span = 2 * MOBA_BLOCK
    lane = lax.broadcasted_iota(jnp.int32, (tq, LANES), 1)
    blk_row = lax.broadcasted_iota(jnp.int32, (nb, tq), 0)
    own_t = ti * (tq // MOBA_BLOCK) + lax.broadcasted_iota(jnp.int32, (nb, tq), 1) // MOBA_BLOCK
    row = lax.broadcasted_iota(jnp.int32, (tq, span), 0)
    col = lax.broadcasted_iota(jnp.int32, (tq, span), 1)

    q_full = []
    for hd in range(N_HEADS):
        qs = q_ref[:, hd * SLAB:(hd + 1) * SLAB]
        gate_t = _dot_t(km_ref[:, hd * SLAB:(hd + 1) * SLAB].astype(BF16), qs)
        picked = _top_blocks(gate_t, blk_row < own_t, blk_row, 0)
        pen_t = jnp.where(picked | (blk_row == own_t), 0.0, NEG)
        pieces = [pen_t]
        if nb < HEAD_DIM:
            pieces.append(jnp.full((HEAD_DIM - nb, tq), NEG, F32))
        zeros = jnp.zeros((HEAD_DIM, tq), F32)
        pieces = [zeros] + pieces if hd % 2 == 0 else pieces + [zeros]
        pen = jnp.concatenate(pieces, axis=0).T
        q_full.append((qs.astype(F32) + pen).astype(BF16))

    def sweep(start, carry, diag):
        start = pl.multiple_of(start, span)
        new = []
        for pair in range(N_HEADS // 2):
            vb = v_ref[pl.ds(start, span), pair * LANES:(pair + 1) * LANES]
            upd = []
            for hd in (2 * pair, 2 * pair + 1):
                s = _dot_t(q_full[hd], k_ref[pl.ds(start, span), hd * SLAB:(hd + 1) * SLAB])
                if diag is not None:
                    s = jnp.where(col + diag * span <= row, s, NEG)
                s_max = jnp.max(s, axis=1, keepdims=True)
                if carry is None:
                    m_new, alpha = s_max, None
                else:
                    m_old, l_old = carry[3 * pair + (hd % 2)], carry[3 * pair + 2][hd % 2]
                    m_new = jnp.maximum(m_old, s_max)
                    alpha = jnp.exp(m_old - m_new)
                p = jnp.exp(s - m_new)
                l_new = jnp.sum(p, axis=1, keepdims=True)
                if alpha is not None:
                    l_new = alpha * l_old + l_new
                upd.append((m_new, alpha, l_new, p.astype(BF16)))
            pv2 = _dot(jnp.concatenate([upd[0][3], upd[1][3]], axis=0), vb)
            pv = jnp.where(lane < HEAD_DIM, pv2[:tq], pv2[tq:])
            if carry is None:
                acc = pv
            else:
                acc = jnp.where(lane < HEAD_DIM, upd[0][1], upd[1][1]) * carry[3 * pair + 2][2] + pv
            new += [upd[0][0], upd[1][0], (upd[0][2], upd[1][2], acc)]
        return tuple(new)

    state = None
    for d in range(tq // span):
        state = sweep(ti * tq + d * span, state, d)
    state = lax.fori_loop(0, ti * (tq // span), lambda i, c: sweep(i * span, c, None), state)
    for pair in range(N_HEADS // 2):
        l0, l1, acc = state[3 * pair + 2]
        o_ref[:, pair * LANES:(pair + 1) * LANES] = (acc / jnp.where(lane < HEAD_DIM, l0, l1)).astype(o_ref.dtype)


def _attn_prompt(q_aug, k_aug, v_bf, kmean, *, n_seq, seq_len, tq):
    t = q_aug.shape[0]
    nb = seq_len // MOBA_BLOCK
    n_tiles = seq_len // tq
    assert tq % (2 * MOBA_BLOCK) == 0 and seq_len % tq == 0
    km = kmean.reshape(n_seq, nb, N_HEADS * SLAB)
    return pl.pallas_call(
        _attn_prompt_kernel,
        grid=(n_seq, n_tiles),
        in_specs=[
            pl.BlockSpec((tq, N_HEADS * SLAB), lambda b, i: (b * n_tiles + i, 0)),
            _resident((seq_len, N_HEADS * SLAB), lambda b, i: (b, 0)),
            _resident((seq_len, D_ATTN), lambda b, i: (b, 0)),
            pl.BlockSpec((None, nb, N_HEADS * SLAB), lambda b, i: (b, 0, 0)),
        ],
        out_specs=pl.BlockSpec((tq, D_ATTN), lambda b, i: (b * n_tiles + i, 0)),
        out_shape=jax.ShapeDtypeStruct((t, D_ATTN), BF16),
        compiler_params=_cparams(("parallel", "arbitrary")),
    )(q_aug, k_aug, v_bf, km)


def _attn_sample_kernel(pt_ref, q_ref, kn_ref, vn_ref, e_ref, *rest, pages_per_step, n_steps, page_size):
    del pt_ref
    k_refs = rest[:pages_per_step]
    v_refs = rest[pages_per_step:2 * pages_per_step]
    o_ref = rest[2 * pages_per_step]
    qm_scr, s_scr, p_scr, gate_scr, acc_scr, l_scr = rest[2 * pages_per_step + 1:]
    j = pl.program_id(1)
    ls = q_ref.shape[0]
    rows = N_HEADS * ls
    pages_per_block = MOBA_BLOCK // page_size
    lane_d = lax.broadcasted_iota(jnp.int32, (ls, D_ATTN), 1)
    lane_r = lax.broadcasted_iota(jnp.int32, (rows, LANES), 1)

    @pl.when(j == 0)
    def _():
        qa = q_ref[...]
        lane = lax.broadcasted_iota(jnp.int32, (ls, LANES), 1)
        q_nat = jnp.concatenate(
            [jnp.where(lane < HEAD_DIM, qa[:, (2 * p) * SLAB:(2 * p + 1) * SLAB],
                       qa[:, (2 * p + 1) * SLAB:(2 * p + 2) * SLAB]) for p in range(N_HEADS // 2)], axis=1)
        qm_scr[...] = jnp.concatenate(
            [jnp.where(lane_d // HEAD_DIM == hd, q_nat, jnp.zeros_like(q_nat)) for hd in range(N_HEADS)], axis=0)
        gate_scr[...] = jnp.zeros_like(gate_scr)

    @pl.when(j < n_steps)
    def _():
        kt = jnp.concatenate([k_refs[i][...].astype(BF16) for i in range(pages_per_step)], axis=1)
        s = _dot(qm_scr[...], kt)
        col = pl.multiple_of(j * (pages_per_step * page_size), pages_per_step * page_size)
        s_scr[:, pl.ds(col, pages_per_step * page_size)] = s
        gate = gate_scr[...]
        for blk in range(pages_per_step // pages_per_block):
            tot = jnp.sum(s[:, blk * MOBA_BLOCK:(blk + 1) * MOBA_BLOCK], axis=1, keepdims=True)
            gate = gate + jnp.where(lane_r == j * (pages_per_step // pages_per_block) + blk, tot, 0.0)
        gate_scr[...] = gate

    @pl.when(j == n_steps - 1)
    def _():
        qm = qm_scr[...]
        n_past = n_steps * pages_per_step // pages_per_block
        lane = lane_r
        gate = gate_scr[...] * (1.0 / MOBA_BLOCK)
        picked = _top_blocks(gate, lane < n_past, lane, 1)
        sel = _dot(jnp.where(picked, 1.0, 0.0).astype(BF16), e_ref[...])
        s = jnp.where(sel > 0.5, s_scr[...], NEG)
        pad = jnp.zeros((LANES - ls, D_ATTN), BF16)
        s_own = _dot_t(qm, jnp.concatenate([kn_ref[...].astype(BF16), pad], axis=0))
        qi = lax.broadcasted_iota(jnp.int32, (rows, LANES), 0) % ls
        s_own = jnp.where(lane <= qi, s_own, NEG)
        m = jnp.maximum(jnp.max(s, axis=1, keepdims=True), jnp.max(s_own, axis=1, keepdims=True))
        p = jnp.exp(s - m)
        p_own = jnp.exp(s_own - m)
        l_scr[...] = jnp.sum(p, axis=1, keepdims=True) + jnp.sum(p_own, axis=1, keepdims=True)
        p_scr[...] = p.astype(BF16)
        acc_scr[...] = _dot(p_own.astype(BF16), jnp.concatenate([vn_ref[...].astype(BF16), pad], axis=0))

    @pl.when(j >= n_steps)
    def _():
        vt = jnp.concatenate([v_refs[i][...].astype(BF16) for i in range(pages_per_step)], axis=1)
        col = pl.multiple_of((j - n_steps) * (pages_per_step * page_size), pages_per_step * page_size)
        acc_scr[...] += _dot_t(p_scr[:, pl.ds(col, pages_per_step * page_size)], vt)

    @pl.when(j == 2 * n_steps - 1)
    def _():
        o = acc_scr[...] / l_scr[...]
        out = jnp.zeros((ls, D_ATTN), F32)
        for hd in range(N_HEADS):
            out = out + jnp.where(lane_d // HEAD_DIM == hd, o[hd * ls:(hd + 1) * ls, :], 0.0)
        o_ref[...] = out.astype(o_ref.dtype)


def _attn_sample(q_aug, k_new, v_new, cache_kt, cache_vt, page_table, layer, *, n_seq, seq_len):
    n_pages = page_table.shape[1]
    page_size = cache_kt.shape[3]
    past_len = n_pages * page_size
    assert past_len % MOBA_BLOCK == 0 and MOBA_BLOCK % page_size == 0 and seq_len % 8 == 0
    assert seq_len <= MOBA_BLOCK and past_len // MOBA_BLOCK <= LANES
    pages_per_step = _pick_tile(n_pages, (16, 8, MOBA_BLOCK // page_size))
    n_steps = n_pages // pages_per_step
    rows = N_HEADS * seq_len
    expand = (jnp.arange(past_len)[None, :] // MOBA_BLOCK == jnp.arange(LANES)[:, None]).astype(BF16)

    def k_map(i):
        return lambda b, j, pt: (layer, pt[b, jnp.minimum(j, n_steps - 1) * pages_per_step + i], 0, 0)

    def v_map(i):
        return lambda b, j, pt: (layer, pt[b, jnp.maximum(j - n_steps, 0) * pages_per_step + i], 0, 0)

    page = lambda m: pl.BlockSpec((None, None, D_ATTN, page_size), m)
    grid_spec = pltpu.PrefetchScalarGridSpec(
        num_scalar_prefetch=1,
        grid=(n_seq, 2 * n_steps),
        in_specs=[pl.BlockSpec((seq_len, N_HEADS * SLAB), lambda b, j, pt: (b, 0)),
                  pl.BlockSpec((seq_len, D_ATTN), lambda b, j, pt: (b, 0)),
                  pl.BlockSpec((seq_len, D_ATTN), lambda b, j, pt: (b, 0)),
                  _resident((LANES, past_len), lambda b, j, pt: (0, 0))]
                 + [page(k_map(i)) for i in range(pages_per_step)]
                 + [page(v_map(i)) for i in range(pages_per_step)],
        out_specs=pl.BlockSpec((seq_len, D_ATTN), lambda b, j, pt: (b, 0)),
        scratch_shapes=[pltpu.VMEM((rows, D_ATTN), BF16), pltpu.VMEM((rows, past_len), F32),
                        pltpu.VMEM((rows, past_len), BF16), pltpu.VMEM((rows, LANES), F32),
                        pltpu.VMEM((rows, D_ATTN), F32), pltpu.VMEM((rows, 1), F32)],
    )
    return pl.pallas_call(
        functools.partial(_attn_sample_kernel, pages_per_step=pages_per_step, n_steps=n_steps, page_size=page_size),
        grid_spec=grid_spec,
        out_shape=jax.ShapeDtypeStruct((n_seq * seq_len, D_ATTN), BF16),
        compiler_params=_cparams(("parallel", "arbitrary")),
    )(page_table, q_aug, k_new, v_new, expand, *([cache_kt] * pages_per_step), *([cache_vt] * pages_per_step))


def _transposed_pages(cache):
    depth, n_pool, page_size = cache.shape[:3]
    return jnp.transpose(cache, (0, 1, 3, 4, 2)).reshape(depth, n_pool, D_ATTN, page_size)


def _mix_kernel(x_ref, u_ref, uh_ref, z_ref, zh_ref, a_ref, g_ref,
                cw_ref, cb_ref, lg_ref, lb_ref, wc_ref, pl_ref, ps_ref, wp_ref, wa_ref, wo_ref,
                o_ref, ext_ref, zext_ref, *, prompt, n_pool_prev):
    tl = u_ref.shape[0]
    i = pl.program_id(1)
    d_model = x_ref.shape[1]

    uh, zh = uh_ref[...], zh_ref[...]
    if prompt:
        uh = jnp.where(i == 0, 0.0, uh)
        zh = jnp.where(i == 0, 0.0, zh)
    ext_ref[0:CONV_HALO, :] = uh
    ext_ref[CONV_HALO:, :] = u_ref[...]
    zext_ref[0:POOL_HALO, :] = zh
    z = z_ref[...]
    zext_ref[POOL_HALO:, :] = z

    y = jnp.zeros((tl, D_CONV), F32) + cb_ref[...]
    base = CONV_HALO - (CONV_WIDTH - 1)
    for j in range(CONV_WIDTH):
        y = y + ext_ref[base + j:base + j + tl, :] * cw_ref[j:j + 1, :]
    mu = jnp.mean(y, axis=-1, keepdims=True)
    yc = y - mu
    var = jnp.mean(yc * yc, axis=-1, keepdims=True)
    yn = yc * lax.rsqrt(var + EPS) * lg_ref[...] + lb_ref[...]
    c = _dot(jax.nn.silu(yn).astype(BF16), wc_ref[...])

    pos = i * tl + lax.broadcasted_iota(jnp.int32, (tl, 1), 0)
    parts = []
    for gi, w in enumerate(POOL_WINDOWS):
        lo, hi = gi * POOL_GROUP, (gi + 1) * POOL_GROUP
        tot = z[:, lo:hi]
        for back in range(1, w):
            tot = tot + zext_ref[POOL_HALO - back:POOL_HALO - back + tl, lo:hi]
        cnt = jnp.minimum(w, pos + 1 + n_pool_prev).astype(F32)
        d = tot / cnt - z[:, lo:hi]
        parts.append(_dot(d.astype(BF16), pl_ref[gi]))
    pm = jnp.concatenate(parts, axis=1) * ps_ref[...]
    p = _dot(pm.astype(BF16), wp_ref[...])

    a = _dot(a_ref[...], wa_ref[...])
    merged = (g_ref[:, 0:d_model] * a + g_ref[:, d_model:2 * d_model] * c
              + g_ref[:, 2 * d_model:3 * d_model] * p)
    o_ref[...] = x_ref[...] + _dot(merged.astype(BF16), wo_ref[...])


def _mix(x, u, z, attn, gates, conv_hist, pool_hist, lw, *, prompt, tl):
    n_seq, seq_len, d_model = x.shape
    n_tiles = seq_len // tl
    seq = lambda c: pl.BlockSpec((None, tl, c), lambda b, i: (b, i, 0))
    if prompt:
        uh_arr, zh_arr = u, z
        uh_spec = pl.BlockSpec((None, CONV_HALO, D_CONV),
                               lambda b, i: (b, jnp.maximum(i * (tl // CONV_HALO) - 1, 0), 0))
        zh_spec = pl.BlockSpec((None, POOL_HALO, D_POOL),
                               lambda b, i: (b, jnp.maximum(i * (tl // POOL_HALO) - 1, 0), 0))
        n_pool_prev = 0
    else:
        assert n_tiles == 1
        uh_arr = jnp.pad(conv_hist, ((0, 0), (CONV_HALO - conv_hist.shape[1], 0), (0, 0)))
        zh_arr = jnp.pad(pool_hist, ((0, 0), (POOL_HALO - pool_hist.shape[1], 0), (0, 0)))
        uh_spec = pl.BlockSpec((None, CONV_HALO, D_CONV), lambda b, i: (b, 0, 0))
        zh_spec = pl.BlockSpec((None, POOL_HALO, D_POOL), lambda b, i: (b, 0, 0))
        n_pool_prev = POOL_STATE
    const = lambda shape: _resident(shape, lambda b, i: (0,) * len(shape))
    return pl.pallas_call(
        functools.partial(_mix_kernel, prompt=prompt, n_pool_prev=n_pool_prev),
        grid=(n_seq, n_tiles),
        in_specs=[seq(d_model), seq(D_CONV), uh_spec, seq(D_POOL), zh_spec, seq(D_ATTN), seq(N_BRANCHES * d_model),
                  const((CONV_WIDTH, D_CONV)), const((1, D_CONV)), const((1, D_CONV)), const((1, D_CONV)),
                  const((D_CONV, d_model)), const((len(POOL_WINDOWS), POOL_GROUP, POOL_GROUP)),
                  const((1, D_POOL)), const((D_POOL, d_model)), const((D_ATTN, d_model)),
                  const((d_model, d_model))],
        out_specs=seq(d_model),
        out_shape=jax.ShapeDtypeStruct(x.shape, F32),
        scratch_shapes=[pltpu.VMEM((CONV_HALO + tl, D_CONV), F32), pltpu.VMEM((POOL_HALO + tl, D_POOL), F32)],
        compiler_params=_cparams(("parallel", "arbitrary")),
    )(x, u, uh_arr, z, zh_arr, attn, gates,
      lw["conv_w"], lw["conv_b"], lw["conv_ln_g"], lw["conv_ln_b"], lw["w_conv_out"], lw["pool_lin"],
      lw["pool_scale"], lw["w_pool_out"], lw["w_attn_out"], lw["w_out"])


def _ffn_chunk(d_ff):
    for n in (2, 4, 1):
        if d_ff % (n * LANES) == 0 and d_ff // n <= 2048:
            return d_ff // n
    return d_ff


def _ffn_kernel(x_ref, gain_ref, w1_ref, w3_ref, w2_ref, fg_ref, o_ref, *, chunk, final_norm):
    x = x_ref[...]
    h = _rms(x, gain_ref[...]).astype(BF16)
    acc = x
    for c0 in range(0, w1_ref.shape[1], chunk):
        act = jax.nn.silu(_dot(h, w1_ref[:, c0:c0 + chunk])) * _dot(h, w3_ref[:, c0:c0 + chunk])
        acc = acc + _dot(act.astype(BF16), w2_ref[c0:c0 + chunk, :])
    o_ref[...] = _rms(acc, fg_ref[...]) if final_norm else acc


def _ffn(x2d, gain, w1, w3, w2, final_gain, *, tm):
    t, d_model = x2d.shape
    d_ff = w1.shape[1]
    final_norm = final_gain is not None
    fg = (final_gain if final_norm else gain).reshape(1, d_model)
    row = pl.BlockSpec((tm, d_model), lambda i: (i, 0))
    vec = _resident((1, d_model), lambda i: (0, 0))
    return pl.pallas_call(
        functools.partial(_ffn_kernel, chunk=_ffn_chunk(d_ff), final_norm=final_norm),
        grid=(t // tm,),
        in_specs=[row, vec, _resident((d_model, d_ff), lambda i: (0, 0)), _resident((d_model, d_ff), lambda i: (0, 0)),
                  _resident((d_ff, d_model), lambda i: (0, 0)), vec],
        out_specs=row,
        out_shape=jax.ShapeDtypeStruct((t, d_model), F32),
        compiler_params=_cparams(("parallel",)),
    )(x2d, gain.reshape(1, d_model), w1, w3, w2, fg)


MOE_CHUNK = 256
MOE_TAIL = LANES


def _dot_halves(a_ref, b):
    half = a_ref.shape[0] // 2
    return jnp.concatenate([_dot(a_ref[:half, :], b), _dot(a_ref[half:, :], b)], axis=0)


def _router_kernel(x_ref, gain_ref, r_ref, tri_ref, rank_ref, rankt_ref, combt_ref, cnt_ref):
    tm = x_ref.shape[0]
    lane = lax.broadcasted_iota(jnp.int32, (tm, LANES), 1)
    h32 = _rms(x_ref[...], gain_ref[...])
    logits = jnp.dot(h32, r_ref[...], preferred_element_type=F32, precision=lax.Precision.HIGHEST)
    logits = jnp.where(lane < N_EXPERTS, logits, -jnp.inf)
    m1 = jnp.max(logits, axis=1, keepdims=True)
    i1 = jnp.min(jnp.where(logits == m1, lane, LANES), axis=1, keepdims=True)
    rest = jnp.where(lane == i1, -jnp.inf, logits)
    m2 = jnp.max(rest, axis=1, keepdims=True)
    i2 = jnp.min(jnp.where(rest == m2, lane, LANES), axis=1, keepdims=True)
    e2 = jnp.exp(m2 - m1)
    den = 1.0 + e2
    comb = jnp.where(lane == i1, 1.0 / den, 0.0) + jnp.where(lane == i2, e2 / den, 0.0)
    routed = (lane == i1) | (lane == i2)
    ones = jnp.where(routed, 1.0, 0.0)
    before = _dot(tri_ref[...], ones.astype(BF16))
    rank = jnp.where(routed, before, -1.0)
    rank_ref[...] = rank
    rankt_ref[...] = rank.T[:N_EXPERTS]
    combt_ref[...] = comb.T[:N_EXPERTS]
    cnt_ref[...] = jnp.sum(ones, axis=0, keepdims=True).astype(jnp.int32)


def _moe_kernel(cnt_ref, x_ref, gain_ref, rank_ref, rankt_ref, combt_ref, w1t_ref, w3t_ref, w2t_ref, fg_ref,
                *refs, final_norm, n_carried):
    o_ref, ht_scr, acct_scr, yt_scr = refs[n_carried:]
    i, e, f = pl.program_id(0), pl.program_id(1), pl.program_id(2)
    n_e, n_f = pl.num_programs(1), pl.num_programs(2)
    tm = x_ref.shape[0]
    n_tail = (jnp.maximum(cnt_ref[i * LANES + e] - MOE_CHUNK, 0) + MOE_TAIL - 1) // MOE_TAIL

    @pl.when((e == 0) & (f == 0))
    def _():
        ht_scr[...] = _rms(x_ref[...], gain_ref[...]).T.astype(BF16)
        yt_scr[...] = jnp.zeros_like(yt_scr)

    lane = lax.broadcasted_iota(jnp.int32, (tm, LANES), 1)
    rank_col = jnp.sum(jnp.where(lane == e, rank_ref[...], 0.0), axis=1, keepdims=True)

    def expert_pass(base, width):
        cols = pl.ds(base if isinstance(base, int) else pl.multiple_of(base, MOE_TAIL), width)
        slot = lax.broadcasted_iota(jnp.int32, (tm, width), 1) + base
        take = jnp.where(rank_col == slot.astype(F32), 1.0, 0.0).astype(BF16)
        xt = _dot_halves(ht_scr, take).astype(BF16)
        act = jax.nn.silu(_dot(w1t_ref[...], xt)) * _dot(w3t_ref[...], xt)
        part = _dot_halves(w2t_ref, act.astype(BF16))

        @pl.when(f == 0)
        def _():
            acct_scr[:, cols] = part

        @pl.when(f > 0)
        def _():
            acct_scr[:, cols] += part

    expert_pass(0, MOE_CHUNK)

    def tail_pass(c, carry):
        expert_pass(MOE_CHUNK + c * MOE_TAIL, MOE_TAIL)
        return carry

    lax.fori_loop(0, n_tail, tail_pass, 0)

    @pl.when(f == n_f - 1)
    def _():
        rank_row = rankt_ref[pl.ds(e, 1), :]
        comb_row = combt_ref[pl.ds(e, 1), :]

        def scatter_pass(base, width):
            cols = pl.ds(base if isinstance(base, int) else pl.multiple_of(base, MOE_TAIL), width)
            slot = lax.broadcasted_iota(jnp.int32, (width, tm), 0) + base
            put = jnp.where(rank_row == slot.astype(F32), 1.0, 0.0).astype(BF16)
            a = acct_scr[:, cols]
            hi = a.astype(BF16)
            lo = (a - hi.astype(F32)).astype(BF16)
            yt_scr[...] += comb_row * (_dot(hi, put) + _dot(lo, put))

        scatter_pass(0, MOE_CHUNK)

        def tail_scatter(c, carry):
            scatter_pass(MOE_CHUNK + c * MOE_TAIL, MOE_TAIL)
            return carry

        lax.fori_loop(0, n_tail, tail_scatter, 0)

    @pl.when((e == n_e - 1) & (f == n_f - 1))
    def _():
        out = x_ref[...] + yt_scr[...].T
        o_ref[...] = _rms(out, fg_ref[...]) if final_norm else out


def _moe_tiling(t):
    main, rest = 7 * LANES, MOE_CHUNK
    if t > rest and (t - rest) % main == 0:
        return [(main, 0, (t - rest) // main), (rest, (t - rest) // rest, 1)]
    tm = _pick_tile(t, (1024, 512, MOE_CHUNK))
    return [(tm, 0, t // tm)]


def _moe(x2d, gain, router, w1t, w3t, w2t, final_gain, *, tm, tile0, n_tiles, out_carried=None):
    t, d_model = x2d.shape
    n_e, d_ff, _ = w1t.shape
    assert n_e == N_EXPERTS and tm % LANES == 0 and tm >= MOE_CHUNK
    chunk = _ffn_chunk(d_ff)
    n_f = d_ff // chunk
    final_norm = final_gain is not None
    fg = (final_gain if final_norm else gain).reshape(1, d_model)
    gain2 = gain.reshape(1, d_model)
    r_pad = jnp.pad(router, ((0, 0), (0, LANES - n_e)))
    earlier = (jnp.arange(tm)[None, :] < jnp.arange(tm)[:, None]).astype(BF16)

    rank, rank_t, comb_t, counts = pl.pallas_call(
        _router_kernel,
        grid=(n_tiles,),
        in_specs=[pl.BlockSpec((tm, d_model), lambda i: (i + tile0, 0)), _resident((1, d_model), lambda i: (0, 0)),
                  _resident((d_model, LANES), lambda i: (0, 0)), _resident((tm, tm), lambda i: (0, 0))],
        out_specs=[pl.BlockSpec((tm, LANES), lambda i: (i, 0)),
                   pl.BlockSpec((None, N_EXPERTS, tm), lambda i: (i, 0, 0)),
                   pl.BlockSpec((None, N_EXPERTS, tm), lambda i: (i, 0, 0)),
                   pl.BlockSpec((None, 1, LANES), lambda i: (i, 0, 0))],
        out_shape=[jax.ShapeDtypeStruct((n_tiles * tm, LANES), F32),
                   jax.ShapeDtypeStruct((n_tiles, N_EXPERTS, tm), F32),
                   jax.ShapeDtypeStruct((n_tiles, N_EXPERTS, tm), F32),
                   jax.ShapeDtypeStruct((n_tiles, 1, LANES), jnp.int32)],
        compiler_params=_cparams(("parallel",)),
    )(x2d, gain2, r_pad, earlier)

    row = pl.BlockSpec((tm, d_model), lambda i, e, f, cnt: (i + tile0, 0))
    vec = _resident((1, d_model), lambda i, e, f, cnt: (0, 0))
    tile_t = pl.BlockSpec((None, N_EXPERTS, tm), lambda i, e, f, cnt: (i, 0, 0))
    carried = () if out_carried is None else (out_carried,)
    grid_spec = pltpu.PrefetchScalarGridSpec(
        num_scalar_prefetch=1,
        grid=(n_tiles, n_e, n_f),
        in_specs=[row, vec, pl.BlockSpec((tm, LANES), lambda i, e, f, cnt: (i, 0)), tile_t, tile_t,
                  pl.BlockSpec((None, chunk, d_model), lambda i, e, f, cnt: (e, f, 0)),
                  pl.BlockSpec((None, chunk, d_model), lambda i, e, f, cnt: (e, f, 0)),
                  pl.BlockSpec((None, d_model, chunk), lambda i, e, f, cnt: (e, 0, f)), vec]
                 + [pl.BlockSpec(memory_space=pl.ANY)] * len(carried),
        out_specs=row,
        scratch_shapes=[pltpu.VMEM((d_model, tm), BF16), pltpu.VMEM((d_model, tm), F32),
                        pltpu.VMEM((d_model, tm), F32)],
    )
    operands = (counts.reshape(n_tiles * LANES), x2d, gain2, rank, rank_t, comb_t, w1t, w3t, w2t, fg) + carried
    return pl.pallas_call(
        functools.partial(_moe_kernel, final_norm=final_norm, n_carried=len(carried)),
        grid_spec=grid_spec,
        out_shape=jax.ShapeDtypeStruct((t, d_model), F32),
        input_output_aliases={len(operands) - 1: 0} if carried else {},
        compiler_params=_cparams(("parallel", "arbitrary", "arbitrary")),
    )(*operands)


def _pick_tile(n, prefs):
    for p in prefs:
        if n % p == 0:
            return p
    return n


def _forward(x, pos, cache, layers, norm_final, *, prompt):
    n_seq, seq_len, d_model = x.shape
    t = n_seq * seq_len
    depth = len(layers)
    if prompt:
        assert seq_len % MOBA_BLOCK == 0 and seq_len // MOBA_BLOCK <= HEAD_DIM
        tm_in = _pick_tile(seq_len, (512, MOBA_BLOCK))
        tables = _rope_tables(pos)
        tl = _pick_tile(seq_len, (512, 256, 128, 64, 32))
        tm_ffn = _pick_tile(t, (512, 256))
        tm_moe = _pick_tile(t, (1024, 512, 256))
    else:
        tm_in = t
        tables = tuple(jnp.tile(tb, (n_seq, 1)) for tb in _rope_tables(pos))
        tl = seq_len
        tm_ffn = tm_moe = t
        cache_kt, cache_vt, page_table, state_conv, state_pool = cache
    ks, vs, cs, ps = [], [], [], []
    kv_t = None
    for l, lw in enumerate(layers):
        outs = _inproj(x.reshape(t, d_model), lw["norm_mix"], lw["w_in_aug"], tables,
                       prompt=prompt, seq_len=seq_len, tm=tm_in, layer=l, depth=depth, kv_carried=kv_t)
        if prompt:
            q_aug, k_aug, kt_all, vt_all, v_bf, kmean, u, z, gates = outs
            kv_t = (kt_all, vt_all)
            attn = _attn_prompt(q_aug, k_aug, v_bf, kmean, n_seq=n_seq, seq_len=seq_len, tq=2 * MOBA_BLOCK)
            conv_hist = pool_hist = None
        else:
            q_aug, k_new, v_new, u, z, gates = outs
            attn = _attn_sample(q_aug, k_new, v_new, cache_kt, cache_vt, page_table, l,
                                n_seq=n_seq, seq_len=seq_len)
            conv_hist, pool_hist = state_conv[l], state_pool[l]
        r3 = lambda a: a.reshape(n_seq, seq_len, a.shape[-1])
        u3, z3 = r3(u), r3(z)
        x = _mix(x, u3, z3, r3(attn), r3(gates), conv_hist, pool_hist, lw, prompt=prompt, tl=tl)
        final_gain = norm_final if l == depth - 1 else None
        x2d = x.reshape(t, d_model)
        if lw["kind"] == "dense":
            x2d = _ffn(x2d, lw["norm_ffn"], lw["w1"], lw["w3"], lw["w2"], final_gain, tm=tm_ffn)
        else:
            out = None
            for tm_moe, tile0, n_tiles in _moe_tiling(t):
                out = _moe(x2d, lw["norm_ffn"], lw["router"], lw["w1"], lw["w3"], lw["w2"], final_gain,
                           tm=tm_moe, tile0=tile0, n_tiles=n_tiles, out_carried=out)
            x2d = out
        x = x2d.reshape(n_seq, seq_len, d_model)
        if prompt:
            cs.append(u3[:, seq_len - (CONV_WIDTH - 1):])
            ps.append(z3[:, seq_len - POOL_STATE:])
        else:
            ks.append(k_new.reshape(n_seq, seq_len, N_HEADS, HEAD_DIM))
            vs.append(v_new.reshape(n_seq, seq_len, N_HEADS, HEAD_DIM))
            cs.append(jnp.concatenate([conv_hist, u3], axis=1)[:, -(CONV_WIDTH - 1):])
            ps.append(jnp.concatenate([pool_hist, z3], axis=1)[:, -POOL_STATE:])
    if prompt:
        untranspose = lambda a: a.reshape(depth, n_seq, N_HEADS, HEAD_DIM, seq_len).transpose(0, 1, 4, 2, 3)
        k_out, v_out = untranspose(kv_t[0]), untranspose(kv_t[1])
    else:
        k_out, v_out = jnp.stack(ks), jnp.stack(vs)
    return x, k_out, v_out, jnp.stack(cs), jnp.stack(ps)


def kernel(x_prompt, x_sample, cache_k, cache_v, page_table, state_conv, state_pool, norm_mix, norm_ffn, norm_final, w_in, w_attn_out, conv_w, conv_b, conv_ln_g, conv_ln_b, w_conv_out, pool_lin, pool_scale, w_pool_out, w_out, ffn_w1, ffn_w3, ffn_w2, moe_router, moe_w1, moe_w3, moe_w2):
    depth = norm_mix.shape[0]
    layers = []
    for l in range(depth):
        lw = dict(
            norm_mix=norm_mix[l], norm_ffn=norm_ffn[l],
            w_in_aug=_prep_w_in(w_in[l]),
            w_attn_out=w_attn_out[l].astype(BF16), conv_w=conv_w[l], conv_b=conv_b[l].reshape(1, -1),
            conv_ln_g=conv_ln_g[l].reshape(1, -1), conv_ln_b=conv_ln_b[l].reshape(1, -1),
            w_conv_out=w_conv_out[l].astype(BF16), pool_lin=pool_lin[l].astype(BF16),
            pool_scale=pool_scale[l].reshape(1, -1), w_pool_out=w_pool_out[l].astype(BF16),
            w_out=w_out[l].astype(BF16))
        i = l // 2
        if l % 2 == 0:
            lw.update(kind="dense", w1=ffn_w1[i].astype(BF16), w3=ffn_w3[i].astype(BF16), w2=ffn_w2[i].astype(BF16))
        else:
            tr = lambda w: jnp.swapaxes(w, 1, 2).astype(BF16)
            lw.update(kind="moe", router=moe_router[i], w1=tr(moe_w1[i]), w3=tr(moe_w3[i]), w2=tr(moe_w2[i]))
        layers.append(lw)

    past_len = page_table.shape[1] * cache_k.shape[2]
    pos_prompt = jnp.arange(x_prompt.shape[1], dtype=jnp.int32)
    pos_sample = past_len + jnp.arange(x_sample.shape[1], dtype=jnp.int32)
    y_p, k_p, v_p, conv_p, pool_p = _forward(x_prompt, pos_prompt, None, layers, norm_final, prompt=True)
    cache = (_transposed_pages(cache_k), _transposed_pages(cache_v), page_table, state_conv, state_pool)
    y_s, k_s, v_s, conv_s, pool_s = _forward(x_sample, pos_sample, cache, layers, norm_final, prompt=False)
    return (y_p, y_s, k_p, v_p, k_s, v_s, conv_p, conv_s, pool_p, pool_s)
```

```python
import functools
import math

import jax
import jax.numpy as jnp
import numpy as np
from jax import lax
from jax.experimental import pallas as pl
from jax.experimental.pallas import tpu as pltpu

F32 = jnp.float32
BF16 = jnp.bfloat16

N_HEADS = 8
HEAD_DIM = 64
D_ATTN = N_HEADS * HEAD_DIM
ROPE_DIM = HEAD_DIM // 4
ROPE_THETA = 500000.0
MOBA_BLOCK = 256
MOBA_TOPK = 3
D_CONV = 512
CONV_WIDTH = 31
D_POOL = 512
POOL_WINDOWS = (2, 4, 8, 16)
POOL_GROUP = D_POOL // len(POOL_WINDOWS)
POOL_STATE = max(POOL_WINDOWS) - 1
N_BRANCHES = 3
N_EXPERTS = 8
EPS = 1e-6

LANES = 128
SLAB = 2 * HEAD_DIM
CONV_HALO = 32
POOL_HALO = 16
NEG = -1e30
VMEM_LIMIT = 56 * 1024 * 1024


def _cparams(sem):
    return pltpu.CompilerParams(dimension_semantics=sem, vmem_limit_bytes=VMEM_LIMIT)


def _resident(shape, index_map):
    return pl.BlockSpec(shape, index_map, pipeline_mode=pl.Buffered(1))


def _rms(x, gain):
    ms = jnp.mean(x * x, axis=-1, keepdims=True)
    return x * lax.rsqrt(ms + EPS) * gain


def _dot(a, b):
    return jnp.dot(a, b, preferred_element_type=F32)


def _dot_t(a, b):
    return lax.dot_general(a, b, (((1,), (1,)), ((), ())), preferred_element_type=F32)


_C_Q = 0
_C_K = _C_Q + N_HEADS * SLAB
_C_V = _C_K + N_HEADS * SLAB
_C_GLU = _C_V + D_ATTN
_C_PZ = _C_GLU + 2 * D_CONV
_C_G = _C_PZ + D_POOL


def _slab_weights(w):
    d = w.shape[0]
    w = w.reshape(d, N_HEADS, HEAD_DIM)
    z = jnp.zeros_like(w)
    even = jnp.concatenate([w, z], axis=-1)
    odd = jnp.concatenate([z, w], axis=-1)
    is_even = (jnp.arange(N_HEADS) % 2 == 0)[None, :, None]
    return jnp.where(is_even, even, odd).reshape(d, N_HEADS * SLAB)


def _prep_w_in(w_in):
    wq = w_in[:, :D_ATTN] * (HEAD_DIM ** -0.5)
    wk = w_in[:, D_ATTN:2 * D_ATTN]
    rest = w_in[:, 2 * D_ATTN:]
    return jnp.concatenate([_slab_weights(wq), _slab_weights(wk), rest], axis=1).astype(BF16)


def _rope_tables(pos):
    half = ROPE_DIM // 2
    step = -2.0 * math.log(ROPE_THETA) / ROPE_DIM
    step_hi = float(np.float32(step))
    step_lo = float(np.float32(step - step_hi))
    idx = jnp.arange(half, dtype=F32)
    inv_freq = jnp.exp(idx * step_hi + idx * step_lo)
    ang = pos.astype(F32)[:, None] * inv_freq[None, :]
    cos, sin = jnp.cos(ang), jnp.sin(ang)
    n = pos.shape[0]
    ones = jnp.ones((n, HEAD_DIM - ROPE_DIM), F32)
    zeros8 = jnp.zeros((n, half), F32)
    zeros48 = jnp.zeros((n, HEAD_DIM - ROPE_DIM), F32)
    c = jnp.concatenate([cos, cos, ones], axis=1)
    s_up = jnp.concatenate([zeros8, sin, zeros48], axis=1)
    s_dn = jnp.concatenate([-sin, zeros8, zeros48], axis=1)
    tile2 = lambda t: jnp.concatenate([t, t], axis=1)
    return tile2(c), tile2(s_up), tile2(s_dn)


def _inproj_kernel(x_ref, gain_ref, w_ref, cos_ref, sup_ref, sdn_ref, *refs, prompt, tiles_per_seq, n_carried):
    out_refs = refs[n_carried:]
    if prompt:
        qa_ref, ka_ref, k_ref, v_ref, vb_ref, km_ref, u_ref, pz_ref, g_ref = out_refs
    else:
        qa_ref, k_ref, v_ref, u_ref, pz_ref, g_ref = out_refs
    tm = x_ref.shape[0]
    h = _rms(x_ref[...], gain_ref[...]).astype(BF16)
    cosv, sup, sdn = cos_ref[...], sup_ref[...], sdn_ref[...]

    def rope(slab):
        return slab * cosv + pltpu.roll(slab, ROPE_DIM // 2, 1) * sup + pltpu.roll(slab, LANES - ROPE_DIM // 2, 1) * sdn

    q_all = _dot(h, w_ref[:, _C_Q:_C_K])
    for hd in range(N_HEADS):
        qa_ref[:, hd * SLAB:(hd + 1) * SLAB] = rope(q_all[:, hd * SLAB:(hd + 1) * SLAB]).astype(BF16)

    k_all = _dot(h, w_ref[:, _C_K:_C_V])
    lane = lax.broadcasted_iota(jnp.int32, (tm, LANES), 1)
    if prompt:
        row = lax.broadcasted_iota(jnp.int32, (tm, LANES), 0)
        blk = ((pl.program_id(0) % tiles_per_seq) * tm + row) // MOBA_BLOCK
        onehot_even = (lane == blk + HEAD_DIM).astype(F32)
        onehot_odd = (lane == blk).astype(F32)
    for pair in range(N_HEADS // 2):
        k_even = rope(k_all[:, (2 * pair) * SLAB:(2 * pair + 1) * SLAB])
        k_odd = rope(k_all[:, (2 * pair + 1) * SLAB:(2 * pair + 2) * SLAB])
        k_nat = jnp.where(lane < HEAD_DIM, k_even, k_odd)
        if not prompt:
            k_ref[:, pair * LANES:(pair + 1) * LANES] = k_nat
        else:
            k_ref[pair * LANES:(pair + 1) * LANES, :] = k_nat.T
            for hd, k_slab, onehot in ((2 * pair, k_even, onehot_even), (2 * pair + 1, k_odd, onehot_odd)):
                ka_ref[:, hd * SLAB:(hd + 1) * SLAB] = (k_slab + onehot).astype(BF16)
                km = jnp.mean(k_slab.reshape(tm // MOBA_BLOCK, MOBA_BLOCK, SLAB), axis=1)
                km_ref[:, 0, hd * SLAB:(hd + 1) * SLAB] = km

    v_all = _dot(h, w_ref[:, _C_V:_C_GLU])
    if prompt:
        vb_ref[...] = v_all.astype(BF16)
        for c in range(D_ATTN // LANES):
            v_ref[c * LANES:(c + 1) * LANES, :] = v_all[:, c * LANES:(c + 1) * LANES].T
    else:
        v_ref[...] = v_all

    glu = _dot(h, w_ref[:, _C_GLU:_C_PZ])
    u_ref[...] = glu[:, :D_CONV] * jax.nn.sigmoid(glu[:, D_CONV:])
    pz_ref[...] = _dot(h, w_ref[:, _C_PZ:_C_G])
    g_ref[...] = jax.nn.sigmoid(_dot(h, w_ref[:, _C_G:]))


def _inproj(x2d, gain, w_aug, tables, *, prompt, seq_len, tm, layer=0, depth=1, kv_carried=None):
    t, d_model = x2d.shape
    n_tiles = t // tm
    cos_t, sup_t, sdn_t = tables
    n_tab = cos_t.shape[0] // tm
    tiles_per_seq = max(seq_len // tm, 1)
    n_cols = w_aug.shape[1]
    row = lambda c: pl.BlockSpec((tm, c), lambda i: (i, 0))
    tab = pl.BlockSpec((tm, LANES), lambda i: (i % n_tab, 0))
    f = jax.ShapeDtypeStruct
    outs = [(f((t, N_HEADS * SLAB), BF16), row(N_HEADS * SLAB))]
    if prompt:
        outs.append((f((t, N_HEADS * SLAB), BF16), row(N_HEADS * SLAB)))
        kv_t = (f((depth, t // seq_len, D_ATTN, seq_len), F32),
                pl.BlockSpec((None, None, D_ATTN, tm), lambda i: (layer, i // tiles_per_seq, 0, i % tiles_per_seq)))
        outs += [kv_t, kv_t]
    else:
        outs += [(f((t, D_ATTN), F32), row(D_ATTN)), (f((t, D_ATTN), F32), row(D_ATTN))]
    if prompt:
        bpt = tm // MOBA_BLOCK
        outs.append((f((t, D_ATTN), BF16), row(D_ATTN)))
        outs.append((f((t // MOBA_BLOCK, 1, N_HEADS * SLAB), F32),
                     pl.BlockSpec((bpt, 1, N_HEADS * SLAB), lambda i: (i, 0, 0))))
    outs += [(f((t, D_CONV), F32), row(D_CONV)), (f((t, D_POOL), F32), row(D_POOL)),
             (f((t, N_BRANCHES * d_model), F32), row(N_BRANCHES * d_model))]
    carried = tuple(kv_carried) if kv_carried is not None else ()
    n_in = 6
    return pl.pallas_call(
        functools.partial(_inproj_kernel, prompt=prompt, tiles_per_seq=tiles_per_seq, n_carried=len(carried)),
        grid=(n_tiles,),
        in_specs=[row(d_model), _resident((1, d_model), lambda i: (0, 0)),
                  _resident((d_model, n_cols), lambda i: (0, 0)), tab, tab, tab]
                 + [pl.BlockSpec(memory_space=pl.ANY)] * len(carried),
        out_specs=[o[1] for o in outs],
        out_shape=[o[0] for o in outs],
        input_output_aliases={n_in + j: 2 + j for j in range(len(carried))},
        compiler_params=_cparams(("parallel",)),
    )(x2d, gain.reshape(1, d_model), w_aug, cos_t, sup_t, sdn_t, *carried)


def _top_blocks(gate, valid, idx, axis):
    g = jnp.where(valid, gate, -jnp.inf)
    picked = jnp.zeros(gate.shape, jnp.bool_)
    for _ in range(MOBA_TOPK):
        m = jnp.max(g, axis=axis, keepdims=True)
        first = jnp.min(jnp.where(g == m, idx, jnp.int32(1 << 20)), axis=axis, keepdims=True)
        pick = (idx == first) & (m > -jnp.inf)
        picked = picked | pick
        g = jnp.where(pick, -jnp.inf, g)
    return picked


def _attn_prompt_kernel(q_ref, k_ref, v_ref, km_ref, o_ref):
    ti = pl.program_id(1)
    tq = q_ref.shape[0]
    nb = km_ref.shape[0]
    span = 2 * MOBA_BLOCK
    lane = lax.broadcasted_iota(jnp.int32, (tq, LANES), 1)
    blk_row = lax.broadcasted_iota(jnp.int32, (nb, tq), 0)
    own_t = ti * (tq // MOBA_BLOCK) + lax.broadcasted_iota(jnp.int32, (nb, tq), 1) // MOBA_BLOCK
    row = lax.broadcasted_iota(jnp.int32, (tq, span), 0)
    col = lax.broadcasted_iota(jnp.int32, (tq, span), 1)

    q_full = []
    for hd in range(N_HEADS):
        qs = q_ref[:, hd * SLAB:(hd + 1) * SLAB]
        gate_t = _dot_t(km_ref[:, hd * SLAB:(hd + 1) * SLAB].astype(BF16), qs)
        picked = _top_blocks(gate_t, blk_row < own_t, blk_row, 0)
        pen_t = jnp.where(picked | (blk_row == own_t), 0.0, NEG)
        pieces = [pen_t]
        if nb < HEAD_DIM:
            pieces.append(jnp.full((HEAD_DIM - nb, tq), NEG, F32))
        zeros = jnp.zeros((HEAD_DIM, tq), F32)
        pieces = [zeros] + pieces if hd % 2 == 0 else pieces + [zeros]
        pen = jnp.concatenate(pieces, axis=0).T
        q_full.append((qs.astype(F32) + pen).astype(BF16))

    def sweep(start, carry, diag):
        start = pl.multiple_of(start, span)
        new = []
        for pair in range(N_HEADS // 2):
            vb = v_ref[pl.ds(start, span), pair * LANES:(pair + 1) * LANES]
            upd = []
            for hd in (2 * pair, 2 * pair + 1):
                s = _dot_t(q_full[hd], k_ref[pl.ds(start, span), hd * SLAB:(hd + 1) * SLAB])
                if diag is not None:
                    s = jnp.where(col + diag * span <= row, s, NEG)
                s_max = jnp.max(s, axis=1, keepdims=True)
                if carry is None:
                    m_new, alpha = s_max, None
                else:
                    m_old = carry[3 * pair + (hd % 2)]
                    m_new = jnp.maximum(m_old, s_max)
                    alpha = jnp.exp(m_old - m_new)
                upd.append((m_new, alpha, jnp.exp(s - m_new).astype(BF16)))
            v_ext = jnp.concatenate([vb, jnp.ones((span, LANES), BF16)], axis=1)
            pv2 = _dot(jnp.concatenate([upd[0][2], upd[1][2]], axis=0), v_ext)
            pv = jnp.where(lane < HEAD_DIM, pv2[:tq, :LANES], pv2[tq:, :LANES])
            l0, l1 = pv2[:tq, LANES:], pv2[tq:, LANES:]
            if carry is None:
                acc = pv
            else:
                l0 = upd[0][1] * carry[3 * pair + 2][0] + l0
                l1 = upd[1][1] * carry[3 * pair + 2][1] + l1
                acc = jnp.where(lane < HEAD_DIM, upd[0][1], upd[1][1]) * carry[3 * pair + 2][2] + pv
            new += [upd[0][0], upd[1][0], (l0, l1, acc)]
        return tuple(new)

    state = None
    for d in range(tq // span):
        state = sweep(ti * tq + d * span, state, d)
    state = lax.fori_loop(0, ti * (tq // span), lambda i, c: sweep(i * span, c, None), state)
    for pair in range(N_HEADS // 2):
        l0, l1, acc = state[3 * pair + 2]
        o_ref[:, pair * LANES:(pair + 1) * LANES] = (acc / jnp.where(lane < HEAD_DIM, l0, l1)).astype(o_ref.dtype)


def _attn_prompt(q_aug, k_aug, v_bf, kmean, *, n_seq, seq_len, tq):
    t = q_aug.shape[0]
    nb = seq_len // MOBA_BLOCK
    n_tiles = seq_len // tq
    assert tq % (2 * MOBA_BLOCK) == 0 and seq_len % tq == 0
    km = kmean.reshape(n_seq, nb, N_HEADS * SLAB)
    return pl.pallas_call(
        _attn_prompt_kernel,
        grid=(n_seq, n_tiles),
        in_specs=[
            pl.BlockSpec((tq, N_HEADS * SLAB), lambda b, i: (b * n_tiles + i, 0)),
            _resident((seq_len, N_HEADS * SLAB), lambda b, i: (b, 0)),
            _resident((seq_len, D_ATTN), lambda b, i: (b, 0)),
            pl.BlockSpec((None, nb, N_HEADS * SLAB), lambda b, i: (b, 0, 0)),
        ],
        out_specs=pl.BlockSpec((tq, D_ATTN), lambda b, i: (b * n_tiles + i, 0)),
        out_shape=jax.ShapeDtypeStruct((t, D_ATTN), BF16),
        compiler_params=_cparams(("parallel", "arbitrary")),
    )(q_aug, k_aug, v_bf, km)


def _attn_sample_kernel(pt_ref, q_ref, kn_ref, vn_ref, e_ref, kc_ref, vc_ref, o_ref, buf, sem, s_scr, p_scr,
                        *, layer, pages_per_step, n_steps, page_size):
    b = pl.program_id(0)
    n_seq = pl.num_programs(0)
    ls = q_ref.shape[0]
    rows = N_HEADS * ls
    step_keys = pages_per_step * page_size
    pages_per_block = MOBA_BLOCK // page_size
    blocks_per_step = pages_per_step // pages_per_block
    n_fetch = 2 * n_steps
    lane_d = lax.broadcasted_iota(jnp.int32, (ls, D_ATTN), 1)
    lane = lax.broadcasted_iota(jnp.int32, (rows, LANES), 1)

    def page_copy(seq, f, i):
        cache = kc_ref if f < n_steps else vc_ref
        page = pt_ref[seq, (f % n_steps) * pages_per_step + i]
        return pltpu.make_async_copy(cache.at[layer, page], buf.at[f % 2, i], sem.at[f % 2])

    def start_fetch(seq, f):
        for i in range(pages_per_step):
            page_copy(seq, f, i).start()

    def wait_fetch(seq, f):
        for i in range(pages_per_step):
            page_copy(seq, f, i).wait()

    def step_pages(f):
        return jnp.concatenate([buf[f % 2, i].astype(BF16) for i in range(pages_per_step)], axis=1)

    @pl.when(b == 0)
    def _():
        start_fetch(b, 0)

    qa = q_ref[...]
    lane_q = lax.broadcasted_iota(jnp.int32, (ls, LANES), 1)
    q_nat = jnp.concatenate(
        [jnp.where(lane_q < HEAD_DIM, qa[:, (2 * p) * SLAB:(2 * p + 1) * SLAB],
                   qa[:, (2 * p + 1) * SLAB:(2 * p + 2) * SLAB]) for p in range(N_HEADS // 2)], axis=1)
    qm = jnp.concatenate(
        [jnp.where(lane_d // HEAD_DIM == hd, q_nat, jnp.zeros_like(q_nat)) for hd in range(N_HEADS)], axis=0)

    gate = jnp.zeros((rows, LANES), F32)
    for f in range(n_steps):
        start_fetch(b, f + 1)
        wait_fetch(b, f)
        s = _dot(qm, step_pages(f))
        s_scr[:, f * step_keys:(f + 1) * step_keys] = s
        for blk in range(blocks_per_step):
            tot = jnp.sum(s[:, blk * MOBA_BLOCK:(blk + 1) * MOBA_BLOCK], axis=1, keepdims=True)
            gate = gate + jnp.where(lane == f * blocks_per_step + blk, tot, 0.0)

    picked = _top_blocks(gate * (1.0 / MOBA_BLOCK), lane < n_steps * blocks_per_step, lane, 1)
    sel = _dot(jnp.where(picked, 1.0, 0.0).astype(BF16), e_ref[...])
    s = jnp.where(sel > 0.5, s_scr[...], NEG)
    pad = jnp.zeros((LANES - ls, D_ATTN), BF16)
    s_own = _dot_t(qm, jnp.concatenate([kn_ref[...].astype(BF16), pad], axis=0))
    qi = lax.broadcasted_iota(jnp.int32, (rows, LANES), 0) % ls
    s_own = jnp.where(lane <= qi, s_own, NEG)
    m = jnp.maximum(jnp.max(s, axis=1, keepdims=True), jnp.max(s_own, axis=1, keepdims=True))
    p = jnp.exp(s - m)
    p_own = jnp.exp(s_own - m)
    l = jnp.sum(p, axis=1, keepdims=True) + jnp.sum(p_own, axis=1, keepdims=True)
    p_scr[...] = p.astype(BF16)
    acc = _dot(p_own.astype(BF16), jnp.concatenate([vn_ref[...].astype(BF16), pad], axis=0))

    for f in range(n_steps, n_fetch):
        if f + 1 < n_fetch:
            start_fetch(b, f + 1)
        else:
            @pl.when(b + 1 < n_seq)
            def _():
                start_fetch(b + 1, 0)
        wait_fetch(b, f)
        g = f - n_steps
        acc = acc + _dot_t(p_scr[:, g * step_keys:(g + 1) * step_keys], step_pages(f))

    o = acc / l
    out = jnp.zeros((ls, D_ATTN), F32)
    for hd in range(N_HEADS):
        out = out + jnp.where(lane_d // HEAD_DIM == hd, o[hd * ls:(hd + 1) * ls, :], 0.0)
    o_ref[...] = out.astype(o_ref.dtype)


def _attn_sample(q_aug, k_new, v_new, cache_kt, cache_vt, page_table, layer, *, n_seq, seq_len):
    n_pages = page_table.shape[1]
    page_size = cache_kt.shape[3]
    past_len = n_pages * page_size
    assert past_len % MOBA_BLOCK == 0 and MOBA_BLOCK % page_size == 0 and seq_len % 8 == 0
    assert seq_len <= MOBA_BLOCK and past_len // MOBA_BLOCK <= LANES
    pages_per_step = _pick_tile(n_pages, (16, 8, MOBA_BLOCK // page_size))
    n_steps = n_pages // pages_per_step
    rows = N_HEADS * seq_len
    expand = (jnp.arange(past_len)[None, :] // MOBA_BLOCK == jnp.arange(LANES)[:, None]).astype(BF16)

    grid_spec = pltpu.PrefetchScalarGridSpec(
        num_scalar_prefetch=1,
        grid=(n_seq,),
        in_specs=[pl.BlockSpec((seq_len, N_HEADS * SLAB), lambda b, pt: (b, 0)),
                  pl.BlockSpec((seq_len, D_ATTN), lambda b, pt: (b, 0)),
                  pl.BlockSpec((seq_len, D_ATTN), lambda b, pt: (b, 0)),
                  _resident((LANES, past_len), lambda b, pt: (0, 0)),
                  pl.BlockSpec(memory_space=pl.ANY), pl.BlockSpec(memory_space=pl.ANY)],
        out_specs=pl.BlockSpec((seq_len, D_ATTN), lambda b, pt: (b, 0)),
        scratch_shapes=[pltpu.VMEM((2, pages_per_step, D_ATTN, page_size), F32), pltpu.SemaphoreType.DMA((2,)),
                        pltpu.VMEM((rows, past_len), F32), pltpu.VMEM((rows, past_len), BF16)],
    )
    return pl.pallas_call(
        functools.partial(_attn_sample_kernel, layer=layer, pages_per_step=pages_per_step, n_steps=n_steps,
                          page_size=page_size),
        grid_spec=grid_spec,
        out_shape=jax.ShapeDtypeStruct((n_seq * seq_len, D_ATTN), BF16),
        compiler_params=_cparams(("arbitrary",)),
    )(page_table, q_aug, k_new, v_new, expand, cache_kt, cache_vt)


def _transposed_pages(cache):
    depth, n_pool, page_size = cache.shape[:3]
    return jnp.transpose(cache, (0, 1, 3, 4, 2)).reshape(depth, n_pool, D_ATTN, page_size)


def _mix_kernel(x_ref, u_ref, uh_ref, z_ref, zh_ref, a_ref, g_ref,
                cw_ref, cb_ref, lg_ref, lb_ref, wc_ref, pl_ref, ps_ref, wp_ref, wa_ref, wo_ref,
                o_ref, ext_ref, zext_ref, *, prompt, n_pool_prev):
    tl = u_ref.shape[0]
    i = pl.program_id(1)
    d_model = x_ref.shape[1]

    uh, zh = uh_ref[...], zh_ref[...]
    if prompt:
        uh = jnp.where(i == 0, 0.0, uh)
        zh = jnp.where(i == 0, 0.0, zh)
    ext_ref[0:CONV_HALO, :] = uh
    ext_ref[CONV_HALO:, :] = u_ref[...]
    zext_ref[0:POOL_HALO, :] = zh
    z = z_ref[...]
    zext_ref[POOL_HALO:, :] = z

    y = jnp.zeros((tl, D_CONV), F32) + cb_ref[...]
    base = CONV_HALO - (CONV_WIDTH - 1)
    for j in range(CONV_WIDTH):
        y = y + ext_ref[base + j:base + j + tl, :] * cw_ref[j:j + 1, :]
    mu = jnp.mean(y, axis=-1, keepdims=True)
    yc = y - mu
    var = jnp.mean(yc * yc, axis=-1, keepdims=True)
    yn = yc * lax.rsqrt(var + EPS) * lg_ref[...] + lb_ref[...]
    c = _dot(jax.nn.silu(yn).astype(BF16), wc_ref[...])

    pos = i * tl + lax.broadcasted_iota(jnp.int32, (tl, 1), 0)
    parts = []
    for gi, w in enumerate(POOL_WINDOWS):
        lo, hi = gi * POOL_GROUP, (gi + 1) * POOL_GROUP
        tot = z[:, lo:hi]
        for back in range(1, w):
            tot = tot + zext_ref[POOL_HALO - back:POOL_HALO - back + tl, lo:hi]
        cnt = jnp.minimum(w, pos + 1 + n_pool_prev).astype(F32)
        d = tot / cnt - z[:, lo:hi]
        parts.append(_dot(d.astype(BF16), pl_ref[gi]))
    pm = jnp.concatenate(parts, axis=1) * ps_ref[...]
    p = _dot(pm.astype(BF16), wp_ref[...])

    a = _dot(a_ref[...], wa_ref[...])
    merged = (g_ref[:, 0:d_model] * a + g_ref[:, d_model:2 * d_model] * c
              + g_ref[:, 2 * d_model:3 * d_model] * p)
    o_ref[...] = x_ref[...] + _dot(merged.astype(BF16), wo_ref[...])


def _mix(x, u, z, attn, gates, conv_hist, pool_hist, lw, *, prompt, tl):
    n_seq, seq_len, d_model = x.shape
    n_tiles = seq_len // tl
    seq = lambda c: pl.BlockSpec((None, tl, c), lambda b, i: (b, i, 0))
    if prompt:
        uh_arr, zh_arr = u, z
        uh_spec = pl.BlockSpec((None, CONV_HALO, D_CONV),
                               lambda b, i: (b, jnp.maximum(i * (tl // CONV_HALO) - 1, 0), 0))
        zh_spec = pl.BlockSpec((None, POOL_HALO, D_POOL),
                               lambda b, i: (b, jnp.maximum(i * (tl // POOL_HALO) - 1, 0), 0))
        n_pool_prev = 0
    else:
        assert n_tiles == 1
        uh_arr = jnp.pad(conv_hist, ((0, 0), (CONV_HALO - conv_hist.shape[1], 0), (0, 0)))
        zh_arr = jnp.pad(pool_hist, ((0, 0), (POOL_HALO - pool_hist.shape[1], 0), (0, 0)))
        uh_spec = pl.BlockSpec((None, CONV_HALO, D_CONV), lambda b, i: (b, 0, 0))
        zh_spec = pl.BlockSpec((None, POOL_HALO, D_POOL), lambda b, i: (b, 0, 0))
        n_pool_prev = POOL_STATE
    const = lambda shape: _resident(shape, lambda b, i: (0,) * len(shape))
    return pl.pallas_call(
        functools.partial(_mix_kernel, prompt=prompt, n_pool_prev=n_pool_prev),
        grid=(n_seq, n_tiles),
        in_specs=[seq(d_model), seq(D_CONV), uh_spec, seq(D_POOL), zh_spec, seq(D_ATTN), seq(N_BRANCHES * d_model),
                  const((CONV_WIDTH, D_CONV)), const((1, D_CONV)), const((1, D_CONV)), const((1, D_CONV)),
                  const((D_CONV, d_model)), const((len(POOL_WINDOWS), POOL_GROUP, POOL_GROUP)),
                  const((1, D_POOL)), const((D_POOL, d_model)), const((D_ATTN, d_model)),
                  const((d_model, d_model))],
        out_specs=seq(d_model),
        out_shape=jax.ShapeDtypeStruct(x.shape, F32),
        scratch_shapes=[pltpu.VMEM((CONV_HALO + tl, D_CONV), F32), pltpu.VMEM((POOL_HALO + tl, D_POOL), F32)],
        compiler_params=_cparams(("parallel", "arbitrary")),
    )(x, u, uh_arr, z, zh_arr, attn, gates,
      lw["conv_w"], lw["conv_b"], lw["conv_ln_g"], lw["conv_ln_b"], lw["w_conv_out"], lw["pool_lin"],
      lw["pool_scale"], lw["w_pool_out"], lw["w_attn_out"], lw["w_out"])


def _ffn_chunk(d_ff):
    for n in (2, 4, 1):
        if d_ff % (n * LANES) == 0 and d_ff // n <= 2048:
            return d_ff // n
    return d_ff


def _ffn_kernel(x_ref, gain_ref, w1_ref, w3_ref, w2_ref, fg_ref, o_ref, *, chunk, final_norm):
    x = x_ref[...]
    h = _rms(x, gain_ref[...]).astype(BF16)
    acc = x
    for c0 in range(0, w1_ref.shape[1], chunk):
        act = jax.nn.silu(_dot(h, w1_ref[:, c0:c0 + chunk])) * _dot(h, w3_ref[:, c0:c0 + chunk])
        acc = acc + _dot(act.astype(BF16), w2_ref[c0:c0 + chunk, :])
    o_ref[...] = _rms(acc, fg_ref[...]) if final_norm else acc


def _ffn(x2d, gain, w1, w3, w2, final_gain, *, tm):
    t, d_model = x2d.shape
    d_ff = w1.shape[1]
    final_norm = final_gain is not None
    fg = (final_gain if final_norm else gain).reshape(1, d_model)
    row = pl.BlockSpec((tm, d_model), lambda i: (i, 0))
    vec = _resident((1, d_model), lambda i: (0, 0))
    return pl.pallas_call(
        functools.partial(_ffn_kernel, chunk=_ffn_chunk(d_ff), final_norm=final_norm),
        grid=(t // tm,),
        in_specs=[row, vec, _resident((d_model, d_ff), lambda i: (0, 0)), _resident((d_model, d_ff), lambda i: (0, 0)),
                  _resident((d_ff, d_model), lambda i: (0, 0)), vec],
        out_specs=row,
        out_shape=jax.ShapeDtypeStruct((t, d_model), F32),
        compiler_params=_cparams(("parallel",)),
    )(x2d, gain.reshape(1, d_model), w1, w3, w2, fg)


MOE_CHUNK = 256
MOE_TAIL = LANES


def _dot_halves(a_ref, b):
    half = a_ref.shape[0] // 2
    return jnp.concatenate([_dot(a_ref[:half, :], b), _dot(a_ref[half:, :], b)], axis=0)


def _router_kernel(x_ref, gain_ref, r_ref, tri_ref, rank_ref, rankt_ref, combt_ref, cnt_ref):
    tm = x_ref.shape[0]
    lane = lax.broadcasted_iota(jnp.int32, (tm, LANES), 1)
    h32 = _rms(x_ref[...], gain_ref[...])
    logits = jnp.dot(h32, r_ref[...], preferred_element_type=F32, precision=lax.Precision.HIGHEST)
    logits = jnp.where(lane < N_EXPERTS, logits, -jnp.inf)
    m1 = jnp.max(logits, axis=1, keepdims=True)
    i1 = jnp.min(jnp.where(logits == m1, lane, LANES), axis=1, keepdims=True)
    rest = jnp.where(lane == i1, -jnp.inf, logits)
    m2 = jnp.max(rest, axis=1, keepdims=True)
    i2 = jnp.min(jnp.where(rest == m2, lane, LANES), axis=1, keepdims=True)
    e2 = jnp.exp(m2 - m1)
    den = 1.0 + e2
    comb = jnp.where(lane == i1, 1.0 / den, 0.0) + jnp.where(lane == i2, e2 / den, 0.0)
    routed = (lane == i1) | (lane == i2)
    ones = jnp.where(routed, 1.0, 0.0)
    before = _dot(tri_ref[...], ones.astype(BF16))
    rank = jnp.where(routed, before, -1.0)
    rank_ref[...] = rank
    rankt_ref[...] = rank.T[:N_EXPERTS]
    combt_ref[...] = comb.T[:N_EXPERTS]
    cnt_ref[...] = jnp.sum(ones, axis=0, keepdims=True).astype(jnp.int32)


def _moe_kernel(cnt_ref, x_ref, gain_ref, rank_ref, rankt_ref, combt_ref, w1t_ref, w3t_ref, w2t_ref, fg_ref,
                *refs, final_norm, n_carried):
    o_ref, ht_scr, acct_scr, yt_scr = refs[n_carried:]
    i, e, f = pl.program_id(0), pl.program_id(1), pl.program_id(2)
    n_e, n_f = pl.num_programs(1), pl.num_programs(2)
    tm = x_ref.shape[0]
    n_tail = (jnp.maximum(cnt_ref[i * LANES + e] - MOE_CHUNK, 0) + MOE_TAIL - 1) // MOE_TAIL

    @pl.when((e == 0) & (f == 0))
    def _():
        ht_scr[...] = _rms(x_ref[...], gain_ref[...]).T.astype(BF16)
        yt_scr[...] = jnp.zeros_like(yt_scr)

    lane = lax.broadcasted_iota(jnp.int32, (tm, LANES), 1)
    rank_col = jnp.sum(jnp.where(lane == e, rank_ref[...], 0.0), axis=1, keepdims=True)

    def expert_pass(base, width):
        cols = pl.ds(base if isinstance(base, int) else pl.multiple_of(base, MOE_TAIL), width)
        slot = lax.broadcasted_iota(jnp.int32, (tm, width), 1) + base
        take = jnp.where(rank_col == slot.astype(F32), 1.0, 0.0).astype(BF16)
        xt = _dot_halves(ht_scr, take).astype(BF16)
        act = jax.nn.silu(_dot(w1t_ref[...], xt)) * _dot(w3t_ref[...], xt)
        part = _dot_halves(w2t_ref, act.astype(BF16))

        @pl.when(f == 0)
        def _():
            acct_scr[:, cols] = part

        @pl.when(f > 0)
        def _():
            acct_scr[:, cols] += part

    expert_pass(0, MOE_CHUNK)

    def tail_pass(c, carry):
        expert_pass(MOE_CHUNK + c * MOE_TAIL, MOE_TAIL)
        return carry

    lax.fori_loop(0, n_tail, tail_pass, 0)

    @pl.when(f == n_f - 1)
    def _():
        rank_row = rankt_ref[pl.ds(e, 1), :]
        comb_row = combt_ref[pl.ds(e, 1), :]

        def scatter_pass(base, width):
            cols = pl.ds(base if isinstance(base, int) else pl.multiple_of(base, MOE_TAIL), width)
            slot = lax.broadcasted_iota(jnp.int32, (width, tm), 0) + base
            put = jnp.where(rank_row == slot.astype(F32), 1.0, 0.0).astype(BF16)
            a = acct_scr[:, cols]
            hi = a.astype(BF16)
            lo = (a - hi.astype(F32)).astype(BF16)
            yt_scr[...] += comb_row * (_dot(hi, put) + _dot(lo, put))

        scatter_pass(0, MOE_CHUNK)

        def tail_scatter(c, carry):
            scatter_pass(MOE_CHUNK + c * MOE_TAIL, MOE_TAIL)
            return carry

        lax.fori_loop(0, n_tail, tail_scatter, 0)

    @pl.when((e == n_e - 1) & (f == n_f - 1))
    def _():
        out = x_ref[...] + yt_scr[...].T
        o_ref[...] = _rms(out, fg_ref[...]) if final_norm else out


def _moe_tiling(t):
    main, rest = 7 * LANES, MOE_CHUNK
    if t > rest and (t - rest) % main == 0:
        return [(main, 0, (t - rest) // main), (rest, (t - rest) // rest, 1)]
    tm = _pick_tile(t, (1024, 512, MOE_CHUNK))
    return [(tm, 0, t // tm)]


def _moe(x2d, gain, router, w1t, w3t, w2t, final_gain, *, tm, tile0, n_tiles, out_carried=None):
    t, d_model = x2d.shape
    n_e, d_ff, _ = w1t.shape
    assert n_e == N_EXPERTS and tm % LANES == 0 and tm >= MOE_CHUNK
    chunk = _ffn_chunk(d_ff)
    n_f = d_ff // chunk
    final_norm = final_gain is not None
    fg = (final_gain if final_norm else gain).reshape(1, d_model)
    gain2 = gain.reshape(1, d_model)
    r_pad = jnp.pad(router, ((0, 0), (0, LANES - n_e)))
    earlier = (jnp.arange(tm)[None, :] < jnp.arange(tm)[:, None]).astype(BF16)

    rank, rank_t, comb_t, counts = pl.pallas_call(
        _router_kernel,
        grid=(n_tiles,),
        in_specs=[pl.BlockSpec((tm, d_model), lambda i: (i + tile0, 0)), _resident((1, d_model), lambda i: (0, 0)),
                  _resident((d_model, LANES), lambda i: (0, 0)), _resident((tm, tm), lambda i: (0, 0))],
        out_specs=[pl.BlockSpec((tm, LANES), lambda i: (i, 0)),
                   pl.BlockSpec((None, N_EXPERTS, tm), lambda i: (i, 0, 0)),
                   pl.BlockSpec((None, N_EXPERTS, tm), lambda i: (i, 0, 0)),
                   pl.BlockSpec((None, 1, LANES), lambda i: (i, 0, 0))],
        out_shape=[jax.ShapeDtypeStruct((n_tiles * tm, LANES), F32),
                   jax.ShapeDtypeStruct((n_tiles, N_EXPERTS, tm), F32),
                   jax.ShapeDtypeStruct((n_tiles, N_EXPERTS, tm), F32),
                   jax.ShapeDtypeStruct((n_tiles, 1, LANES), jnp.int32)],
        compiler_params=_cparams(("parallel",)),
    )(x2d, gain2, r_pad, earlier)

    row = pl.BlockSpec((tm, d_model), lambda i, e, f, cnt: (i + tile0, 0))
    vec = _resident((1, d_model), lambda i, e, f, cnt: (0, 0))
    tile_t = pl.BlockSpec((None, N_EXPERTS, tm), lambda i, e, f, cnt: (i, 0, 0))
    carried = () if out_carried is None else (out_carried,)
    grid_spec = pltpu.PrefetchScalarGridSpec(
        num_scalar_prefetch=1,
        grid=(n_tiles, n_e, n_f),
        in_specs=[row, vec, pl.BlockSpec((tm, LANES), lambda i, e, f, cnt: (i, 0)), tile_t, tile_t,
                  pl.BlockSpec((None, chunk, d_model), lambda i, e, f, cnt: (e, f, 0)),
                  pl.BlockSpec((None, chunk, d_model), lambda i, e, f, cnt: (e, f, 0)),
                  pl.BlockSpec((None, d_model, chunk), lambda i, e, f, cnt: (e, 0, f)), vec]
                 + [pl.BlockSpec(memory_space=pl.ANY)] * len(carried),
        out_specs=row,
        scratch_shapes=[pltpu.VMEM((d_model, tm), BF16), pltpu.VMEM((d_model, tm), F32),
                        pltpu.VMEM((d_model, tm), F32)],
    )
    operands = (counts.reshape(n_tiles * LANES), x2d, gain2, rank, rank_t, comb_t, w1t, w3t, w2t, fg) + carried
    return pl.pallas_call(
        functools.partial(_moe_kernel, final_norm=final_norm, n_carried=len(carried)),
        grid_spec=grid_spec,
        out_shape=jax.ShapeDtypeStruct((t, d_model), F32),
        input_output_aliases={len(operands) - 1: 0} if carried else {},
        compiler_params=_cparams(("parallel", "arbitrary", "arbitrary")),
    )(*operands)


def _pick_tile(n, prefs):
    for p in prefs:
        if n % p == 0:
            return p
    return n


def _forward(x, pos, cache, layers, norm_final, *, prompt):
    n_seq, seq_len, d_model = x.shape
    t = n_seq * seq_len
    depth = len(layers)
    if prompt:
        assert seq_len % MOBA_BLOCK == 0 and seq_len // MOBA_BLOCK <= HEAD_DIM
        tm_in = _pick_tile(seq_len, (512, MOBA_BLOCK))
        tables = _rope_tables(pos)
        tl = _pick_tile(seq_len, (512, 256, 128, 64, 32))
        tm_ffn = _pick_tile(t, (512, 256))
        tm_moe = _pick_tile(t, (1024, 512, 256))
    else:
        tm_in = t
        tables = tuple(jnp.tile(tb, (n_seq, 1)) for tb in _rope_tables(pos))
        tl = seq_len
        tm_ffn = tm_moe = t
        cache_kt, cache_vt, page_table, state_conv, state_pool = cache
    ks, vs, cs, ps = [], [], [], []
    kv_t = None
    for l, lw in enumerate(layers):
        outs = _inproj(x.reshape(t, d_model), lw["norm_mix"], lw["w_in_aug"], tables,
                       prompt=prompt, seq_len=seq_len, tm=tm_in, layer=l, depth=depth, kv_carried=kv_t)
        if prompt:
            q_aug, k_aug, kt_all, vt_all, v_bf, kmean, u, z, gates = outs
            kv_t = (kt_all, vt_all)
            attn = _attn_prompt(q_aug, k_aug, v_bf, kmean, n_seq=n_seq, seq_len=seq_len, tq=2 * MOBA_BLOCK)
            conv_hist = pool_hist = None
        else:
            q_aug, k_new, v_new, u, z, gates = outs
            attn = _attn_sample(q_aug, k_new, v_new, cache_kt, cache_vt, page_table, l,
                                n_seq=n_seq, seq_len=seq_len)
            conv_hist, pool_hist = state_conv[l], state_pool[l]
        r3 = lambda a: a.reshape(n_seq, seq_len, a.shape[-1])
        u3, z3 = r3(u), r3(z)
        x = _mix(x, u3, z3, r3(attn), r3(gates), conv_hist, pool_hist, lw, prompt=prompt, tl=tl)
        final_gain = norm_final if l == depth - 1 else None
        x2d = x.reshape(t, d_model)
        if lw["kind"] == "dense":
            x2d = _ffn(x2d, lw["norm_ffn"], lw["w1"], lw["w3"], lw["w2"], final_gain, tm=tm_ffn)
        else:
            out = None
            for tm_moe, tile0, n_tiles in _moe_tiling(t):
                out = _moe(x2d, lw["norm_ffn"], lw["router"], lw["w1"], lw["w3"], lw["w2"], final_gain,
                           tm=tm_moe, tile0=tile0, n_tiles=n_tiles, out_carried=out)
            x2d = out
        x = x2d.reshape(n_seq, seq_len, d_model)
        if prompt:
            cs.append(u3[:, seq_len - (CONV_WIDTH - 1):])
            ps.append(z3[:, seq_len - POOL_STATE:])
        else:
            ks.append(k_new.reshape(n_seq, seq_len, N_HEADS, HEAD_DIM))
            vs.append(v_new.reshape(n_seq, seq_len, N_HEADS, HEAD_DIM))
            cs.append(jnp.concatenate([conv_hist, u3], axis=1)[:, -(CONV_WIDTH - 1):])
            ps.append(jnp.concatenate([pool_hist, z3], axis=1)[:, -POOL_STATE:])
    if prompt:
        untranspose = lambda a: a.reshape(depth, n_seq, N_HEADS, HEAD_DIM, seq_len).transpose(0, 1, 4, 2, 3)
        k_out, v_out = untranspose(kv_t[0]), untranspose(kv_t[1])
    else:
        k_out, v_out = jnp.stack(ks), jnp.stack(vs)
    return x, k_out, v_out, jnp.stack(cs), jnp.stack(ps)


def kernel(x_prompt, x_sample, cache_k, cache_v, page_table, state_conv, state_pool, norm_mix, norm_ffn, norm_final, w_in, w_attn_out, conv_w, conv_b, conv_ln_g, conv_ln_b, w_conv_out, pool_lin, pool_scale, w_pool_out, w_out, ffn_w1, ffn_w3, ffn_w2, moe_router, moe_w1, moe_w3, moe_w2):
    depth = norm_mix.shape[0]
    layers = []
    for l in range(depth):
        lw = dict(
            norm_mix=norm_mix[l], norm_ffn=norm_ffn[l],
            w_in_aug=_prep_w_in(w_in[l]),
            w_attn_out=w_attn_out[l].astype(BF16), conv_w=conv_w[l], conv_b=conv_b[l].reshape(1, -1),
            conv_ln_g=conv_ln_g[l].reshape(1, -1), conv_ln_b=conv_ln_b[l].reshape(1, -1),
            w_conv_out=w_conv_out[l].astype(BF16), pool_lin=pool_lin[l].astype(BF16),
            pool_scale=pool_scale[l].reshape(1, -1), w_pool_out=w_pool_out[l].astype(BF16),
            w_out=w_out[l].astype(BF16))
        i = l // 2
        if l % 2 == 0:
            lw.update(kind="dense", w1=ffn_w1[i].astype(BF16), w3=ffn_w3[i].astype(BF16), w2=ffn_w2[i].astype(BF16))
        else:
            tr = lambda w: jnp.swapaxes(w, 1, 2).astype(BF16)
            lw.update(kind="moe", router=moe_router[i], w1=tr(moe_w1[i]), w3=tr(moe_w3[i]), w2=tr(moe_w2[i]))
        layers.append(lw)

    past_len = page_table.shape[1] * cache_k.shape[2]
    pos_prompt = jnp.arange(x_prompt.shape[1], dtype=jnp.int32)
    pos_sample = past_len + jnp.arange(x_sample.shape[1], dtype=jnp.int32)
    y_p, k_p, v_p, conv_p, pool_p = _forward(x_prompt, pos_prompt, None, layers, norm_final, prompt=True)
    cache = (_transposed_pages(cache_k), _transposed_pages(cache_v), page_table, state_conv, state_pool)
    y_s, k_s, v_s, conv_s, pool_s = _forward(x_sample, pos_sample, cache, layers, norm_final, prompt=False)
    return (y_p, y_s, k_p, v_p, k_s, v_s, conv_p, conv_s, pool_p, pool_s)
```

```python
import functools
import math

import jax
import jax.numpy as jnp
import numpy as np
from jax import lax
from jax.experimental import pallas as pl
from jax.experimental.pallas import tpu as pltpu

F32 = jnp.float32
BF16 = jnp.bfloat16

N_HEADS = 8
HEAD_DIM = 64
D_ATTN = N_HEADS * HEAD_DIM
ROPE_DIM = HEAD_DIM // 4
ROPE_THETA = 500000.0
MOBA_BLOCK = 256
MOBA_TOPK = 3
D_CONV = 512
CONV_WIDTH = 31
D_POOL = 512
POOL_WINDOWS = (2, 4, 8, 16)
POOL_GROUP = D_POOL // len(POOL_WINDOWS)
POOL_STATE = max(POOL_WINDOWS) - 1
N_BRANCHES = 3
N_EXPERTS = 8
EPS = 1e-6

LANES = 128
SUBLANES = 8
SLAB = 2 * HEAD_DIM
CONV_HALO = 32
POOL_HALO = 16
NEG = -1e30
VMEM_LIMIT = 56 * 1024 * 1024


def _cparams(sem):
    return pltpu.CompilerParams(dimension_semantics=sem, vmem_limit_bytes=VMEM_LIMIT)


def _resident(shape, index_map):
    return pl.BlockSpec(shape, index_map, pipeline_mode=pl.Buffered(1))


def _rms(x, gain):
    ms = jnp.mean(x * x, axis=-1, keepdims=True)
    return x * lax.rsqrt(ms + EPS) * gain


def _dot(a, b):
    return jnp.dot(a, b, preferred_element_type=F32)


def _dot_t(a, b):
    return lax.dot_general(a, b, (((1,), (1,)), ((), ())), preferred_element_type=F32)


_C_Q = 0
_C_K = _C_Q + N_HEADS * SLAB
_C_V = _C_K + N_HEADS * SLAB
_C_GLU = _C_V + D_ATTN
_C_PZ = _C_GLU + 2 * D_CONV
_C_G = _C_PZ + D_POOL


def _slab_weights(w):
    d = w.shape[0]
    w = w.reshape(d, N_HEADS, HEAD_DIM)
    z = jnp.zeros_like(w)
    even = jnp.concatenate([w, z], axis=-1)
    odd = jnp.concatenate([z, w], axis=-1)
    is_even = (jnp.arange(N_HEADS) % 2 == 0)[None, :, None]
    return jnp.where(is_even, even, odd).reshape(d, N_HEADS * SLAB)


def _prep_w_in(w_in):
    wq = w_in[:, :D_ATTN] * (HEAD_DIM ** -0.5)
    wk = w_in[:, D_ATTN:2 * D_ATTN]
    rest = w_in[:, 2 * D_ATTN:]
    return jnp.concatenate([_slab_weights(wq), _slab_weights(wk), rest], axis=1).astype(BF16)


def _rope_tables(pos):
    half = ROPE_DIM // 2
    step = -2.0 * math.log(ROPE_THETA) / ROPE_DIM
    step_hi = float(np.float32(step))
    step_lo = float(np.float32(step - step_hi))
    idx = jnp.arange(half, dtype=F32)
    inv_freq = jnp.exp(idx * step_hi + idx * step_lo)
    ang = pos.astype(F32)[:, None] * inv_freq[None, :]
    cos, sin = jnp.cos(ang), jnp.sin(ang)
    n = pos.shape[0]
    ones = jnp.ones((n, HEAD_DIM - ROPE_DIM), F32)
    zeros8 = jnp.zeros((n, half), F32)
    zeros48 = jnp.zeros((n, HEAD_DIM - ROPE_DIM), F32)
    c = jnp.concatenate([cos, cos, ones], axis=1)
    s_up = jnp.concatenate([zeros8, sin, zeros48], axis=1)
    s_dn = jnp.concatenate([-sin, zeros8, zeros48], axis=1)
    tile2 = lambda t: jnp.concatenate([t, t], axis=1)
    return tile2(c), tile2(s_up), tile2(s_dn)


def _inproj_kernel(x_ref, gain_ref, w_ref, cos_ref, sup_ref, sdn_ref, *refs, prompt, tiles_per_seq, n_carried):
    out_refs = refs[n_carried:]
    if prompt:
        qa_ref, ka_ref, k_ref, v_ref, vb_ref, km_ref, u_ref, pz_ref, g_ref = out_refs
    else:
        qa_ref, k_ref, v_ref, u_ref, pz_ref, g_ref = out_refs
    tm = x_ref.shape[0]
    h = _rms(x_ref[...], gain_ref[...]).astype(BF16)
    cosv, sup, sdn = cos_ref[...], sup_ref[...], sdn_ref[...]

    def rope(slab):
        return slab * cosv + pltpu.roll(slab, ROPE_DIM // 2, 1) * sup + pltpu.roll(slab, LANES - ROPE_DIM // 2, 1) * sdn

    q_all = _dot(h, w_ref[:, _C_Q:_C_K])
    for hd in range(N_HEADS):
        qa_ref[:, hd * SLAB:(hd + 1) * SLAB] = rope(q_all[:, hd * SLAB:(hd + 1) * SLAB]).astype(BF16)

    k_all = _dot(h, w_ref[:, _C_K:_C_V])
    lane = lax.broadcasted_iota(jnp.int32, (tm, LANES), 1)
    if prompt:
        row = lax.broadcasted_iota(jnp.int32, (tm, LANES), 0)
        blk = ((pl.program_id(0) % tiles_per_seq) * tm + row) // MOBA_BLOCK
        onehot_even = (lane == blk + HEAD_DIM).astype(F32)
        onehot_odd = (lane == blk).astype(F32)
    for pair in range(N_HEADS // 2):
        k_even = rope(k_all[:, (2 * pair) * SLAB:(2 * pair + 1) * SLAB])
        k_odd = rope(k_all[:, (2 * pair + 1) * SLAB:(2 * pair + 2) * SLAB])
        k_nat = jnp.where(lane < HEAD_DIM, k_even, k_odd)
        if not prompt:
            k_ref[:, pair * LANES:(pair + 1) * LANES] = k_nat
        else:
            k_ref[pair * LANES:(pair + 1) * LANES, :] = k_nat.T
            for hd, k_slab, onehot in ((2 * pair, k_even, onehot_even), (2 * pair + 1, k_odd, onehot_odd)):
                ka_ref[:, hd * SLAB:(hd + 1) * SLAB] = (k_slab + onehot).astype(BF16)
                km = jnp.mean(k_slab.reshape(tm // MOBA_BLOCK, MOBA_BLOCK, SLAB), axis=1)
                km_ref[:, 0, hd * SLAB:(hd + 1) * SLAB] = km

    v_all = _dot(h, w_ref[:, _C_V:_C_GLU])
    if prompt:
        vb_ref[...] = v_all.astype(BF16)
        for c in range(D_ATTN // LANES):
            v_ref[c * LANES:(c + 1) * LANES, :] = v_all[:, c * LANES:(c + 1) * LANES].T
    else:
        v_ref[...] = v_all

    glu = _dot(h, w_ref[:, _C_GLU:_C_PZ])
    u_ref[...] = glu[:, :D_CONV] * jax.nn.sigmoid(glu[:, D_CONV:])
    pz_ref[...] = _dot(h, w_ref[:, _C_PZ:_C_G])
    g_ref[...] = jax.nn.sigmoid(_dot(h, w_ref[:, _C_G:]))


def _inproj(x2d, gain, w_aug, tables, *, prompt, seq_len, tm, layer=0, depth=1, kv_carried=None):
    t, d_model = x2d.shape
    n_tiles = t // tm
    cos_t, sup_t, sdn_t = tables
    n_tab = cos_t.shape[0] // tm
    tiles_per_seq = max(seq_len // tm, 1)
    n_cols = w_aug.shape[1]
    row = lambda c: pl.BlockSpec((tm, c), lambda i: (i, 0))
    tab = pl.BlockSpec((tm, LANES), lambda i: (i % n_tab, 0))
    f = jax.ShapeDtypeStruct
    outs = [(f((t, N_HEADS * SLAB), BF16), row(N_HEADS * SLAB))]
    if prompt:
        outs.append((f((t, N_HEADS * SLAB), BF16), row(N_HEADS * SLAB)))
        kv_t = (f((depth, t // seq_len, D_ATTN, seq_len), F32),
                pl.BlockSpec((None, None, D_ATTN, tm), lambda i: (layer, i // tiles_per_seq, 0, i % tiles_per_seq)))
        outs += [kv_t, kv_t]
    else:
        outs += [(f((t, D_ATTN), F32), row(D_ATTN)), (f((t, D_ATTN), F32), row(D_ATTN))]
    if prompt:
        bpt = tm // MOBA_BLOCK
        outs.append((f((t, D_ATTN), BF16), row(D_ATTN)))
        outs.append((f((t // MOBA_BLOCK, 1, N_HEADS * SLAB), F32),
                     pl.BlockSpec((bpt, 1, N_HEADS * SLAB), lambda i: (i, 0, 0))))
    outs += [(f((t, D_CONV), F32), row(D_CONV)), (f((t, D_POOL), F32), row(D_POOL)),
             (f((t, N_BRANCHES * d_model), F32), row(N_BRANCHES * d_model))]
    carried = tuple(kv_carried) if kv_carried is not None else ()
    n_in = 6
    return pl.pallas_call(
        functools.partial(_inproj_kernel, prompt=prompt, tiles_per_seq=tiles_per_seq, n_carried=len(carried)),
        grid=(n_tiles,),
        in_specs=[row(d_model), _resident((1, d_model), lambda i: (0, 0)),
                  _resident((d_model, n_cols), lambda i: (0, 0)), tab, tab, tab]
                 + [pl.BlockSpec(memory_space=pl.ANY)] * len(carried),
        out_specs=[o[1] for o in outs],
        out_shape=[o[0] for o in outs],
        input_output_aliases={n_in + j: 2 + j for j in range(len(carried))},
        compiler_params=_cparams(("parallel",)),
    )(x2d, gain.reshape(1, d_model), w_aug, cos_t, sup_t, sdn_t, *carried)


def _top_blocks(gate, valid, idx, axis):
    g = jnp.where(valid, gate, -jnp.inf)
    picked = jnp.zeros(gate.shape, jnp.bool_)
    for _ in range(MOBA_TOPK):
        m = jnp.max(g, axis=axis, keepdims=True)
        first = jnp.min(jnp.where(g == m, idx, jnp.int32(1 << 20)), axis=axis, keepdims=True)
        pick = (idx == first) & (m > -jnp.inf)
        picked = picked | pick
        g = jnp.where(pick, -jnp.inf, g)
    return picked


def _attn_prompt_kernel(q_ref, k_ref, v_ref, km_ref, o_ref):
    ti = pl.program_id(1)
    tq = q_ref.shape[0]
    nb = km_ref.shape[0]
    span = 2 * MOBA_BLOCK
    lane = lax.broadcasted_iota(jnp.int32, (tq, LANES), 1)
    blk_row = lax.broadcasted_iota(jnp.int32, (nb, tq), 0)
    own_t = ti * (tq // MOBA_BLOCK) + lax.broadcasted_iota(jnp.int32, (nb, tq), 1) // MOBA_BLOCK
    row = lax.broadcasted_iota(jnp.int32, (tq, span), 0)
    col = lax.broadcasted_iota(jnp.int32, (tq, span), 1)

    q_full = []
    for hd in range(N_HEADS):
        qs = q_ref[:, hd * SLAB:(hd + 1) * SLAB]
        gate_t = _dot_t(km_ref[:, hd * SLAB:(hd + 1) * SLAB].astype(BF16), qs)
        picked = _top_blocks(gate_t, blk_row < own_t, blk_row, 0)
        pen_t = jnp.where(picked | (blk_row == own_t), 0.0, NEG)
        pieces = [pen_t]
        if nb < HEAD_DIM:
            pieces.append(jnp.full((HEAD_DIM - nb, tq), NEG, F32))
        zeros = jnp.zeros((HEAD_DIM, tq), F32)
        pieces = [zeros] + pieces if hd % 2 == 0 else pieces + [zeros]
        pen = jnp.concatenate(pieces, axis=0).T
        q_full.append((qs.astype(F32) + pen).astype(BF16))

    def sweep(start, carry, diag, width=span):
        start = start if isinstance(start, int) else pl.multiple_of(start, span)
        new = []
        for pair in range(N_HEADS // 2):
            vb = v_ref[pl.ds(start, width), pair * LANES:(pair + 1) * LANES]
            upd = []
            for hd in (2 * pair, 2 * pair + 1):
                s = _dot_t(q_full[hd], k_ref[pl.ds(start, width), hd * SLAB:(hd + 1) * SLAB])
                if diag is not None:
                    s = jnp.where(col + diag * span <= row, s, NEG)
                s_max = jnp.max(s, axis=1, keepdims=True)
                if carry is None:
                    m_new, alpha = s_max, None
                else:
                    m_old = carry[3 * pair + (hd % 2)]
                    m_new = jnp.maximum(m_old, s_max)
                    alpha = jnp.exp(m_old - m_new)
                upd.append((m_new, alpha, jnp.exp(s - m_new).astype(BF16)))
            v_ext = jnp.concatenate([vb, jnp.ones((width, LANES), BF16)], axis=1)
            pv2 = _dot(jnp.concatenate([upd[0][2], upd[1][2]], axis=0), v_ext)
            pv = jnp.where(lane < HEAD_DIM, pv2[:tq, :LANES], pv2[tq:, :LANES])
            l0, l1 = pv2[:tq, LANES:], pv2[tq:, LANES:]
            if carry is None:
                acc = pv
            else:
                l0 = upd[0][1] * carry[3 * pair + 2][0] + l0
                l1 = upd[1][1] * carry[3 * pair + 2][1] + l1
                acc = jnp.where(lane < HEAD_DIM, upd[0][1], upd[1][1]) * carry[3 * pair + 2][2] + pv
            new += [upd[0][0], upd[1][0], (l0, l1, acc)]
        return tuple(new)

    state = None
    for d in range(tq // span):
        state = sweep(ti * tq + d * span, state, d)
    n_past = ti * (tq // span)
    odd = n_past % 2
    state = lax.fori_loop(0, odd, lambda i, c: sweep(0, c, None), state)
    state = lax.fori_loop(0, n_past // 2, lambda i, c: sweep((odd + 2 * i) * span, c, None, 2 * span), state)
    for pair in range(N_HEADS // 2):
        l0, l1, acc = state[3 * pair + 2]
        o_ref[:, pair * LANES:(pair + 1) * LANES] = (acc / jnp.where(lane < HEAD_DIM, l0, l1)).astype(o_ref.dtype)


def _attn_prompt(q_aug, k_aug, v_bf, kmean, *, n_seq, seq_len, tq):
    t = q_aug.shape[0]
    nb = seq_len // MOBA_BLOCK
    n_tiles = seq_len // tq
    assert tq % (2 * MOBA_BLOCK) == 0 and seq_len % tq == 0
    km = kmean.reshape(n_seq, nb, N_HEADS * SLAB)
    return pl.pallas_call(
        _attn_prompt_kernel,
        grid=(n_seq, n_tiles),
        in_specs=[
            pl.BlockSpec((tq, N_HEADS * SLAB), lambda b, i: (b * n_tiles + i, 0)),
            _resident((seq_len, N_HEADS * SLAB), lambda b, i: (b, 0)),
            _resident((seq_len, D_ATTN), lambda b, i: (b, 0)),
            pl.BlockSpec((None, nb, N_HEADS * SLAB), lambda b, i: (b, 0, 0)),
        ],
        out_specs=pl.BlockSpec((tq, D_ATTN), lambda b, i: (b * n_tiles + i, 0)),
        out_shape=jax.ShapeDtypeStruct((t, D_ATTN), BF16),
        compiler_params=_cparams(("parallel", "arbitrary")),
    )(q_aug, k_aug, v_bf, km)


def _attn_sample_kernel(pt_ref, q_ref, kn_ref, vn_ref, e_ref, kc_ref, vc_ref, o_ref, buf, sem, s_scr, p_scr,
                        *, layer, pages_per_step, n_steps, page_size):
    b = pl.program_id(0)
    n_seq = pl.num_programs(0)
    ls = q_ref.shape[0]
    rows = N_HEADS * ls
    step_keys = pages_per_step * page_size
    pages_per_block = MOBA_BLOCK // page_size
    blocks_per_step = pages_per_step // pages_per_block
    n_fetch = 2 * n_steps
    lane_d = lax.broadcasted_iota(jnp.int32, (ls, D_ATTN), 1)
    lane = lax.broadcasted_iota(jnp.int32, (rows, LANES), 1)

    def page_copy(seq, f, i):
        cache = kc_ref if f < n_steps else vc_ref
        page = pt_ref[seq, (f % n_steps) * pages_per_step + i]
        return pltpu.make_async_copy(cache.at[layer, page], buf.at[f % 2, i], sem.at[f % 2])

    def start_fetch(seq, f):
        for i in range(pages_per_step):
            page_copy(seq, f, i).start()

    def wait_fetch(seq, f):
        for i in range(pages_per_step):
            page_copy(seq, f, i).wait()

    def step_pages(f):
        return jnp.concatenate([buf[f % 2, i].astype(BF16) for i in range(pages_per_step)], axis=1)

    @pl.when(b == 0)
    def _():
        start_fetch(b, 0)

    qa = q_ref[...]
    lane_q = lax.broadcasted_iota(jnp.int32, (ls, LANES), 1)
    q_nat = jnp.concatenate(
        [jnp.where(lane_q < HEAD_DIM, qa[:, (2 * p) * SLAB:(2 * p + 1) * SLAB],
                   qa[:, (2 * p + 1) * SLAB:(2 * p + 2) * SLAB]) for p in range(N_HEADS // 2)], axis=1)
    qm = jnp.concatenate(
        [jnp.where(lane_d // HEAD_DIM == hd, q_nat, jnp.zeros_like(q_nat)) for hd in range(N_HEADS)], axis=0)

    gate = jnp.zeros((rows, LANES), F32)
    for f in range(n_steps):
        start_fetch(b, f + 1)
        wait_fetch(b, f)
        s = _dot(qm, step_pages(f))
        s_scr[:, f * step_keys:(f + 1) * step_keys] = s
        for blk in range(blocks_per_step):
            tot = jnp.sum(s[:, blk * MOBA_BLOCK:(blk + 1) * MOBA_BLOCK], axis=1, keepdims=True)
            gate = gate + jnp.where(lane == f * blocks_per_step + blk, tot, 0.0)

    picked = _top_blocks(gate * (1.0 / MOBA_BLOCK), lane < n_steps * blocks_per_step, lane, 1)
    sel = _dot(jnp.where(picked, 1.0, 0.0).astype(BF16), e_ref[...])
    s = jnp.where(sel > 0.5, s_scr[...], NEG)
    pad = jnp.zeros((LANES - ls, D_ATTN), BF16)
    s_own = _dot_t(qm, jnp.concatenate([kn_ref[...].astype(BF16), pad], axis=0))
    qi = lax.broadcasted_iota(jnp.int32, (rows, LANES), 0) % ls
    s_own = jnp.where(lane <= qi, s_own, NEG)
    m = jnp.maximum(jnp.max(s, axis=1, keepdims=True), jnp.max(s_own, axis=1, keepdims=True))
    p = jnp.exp(s - m)
    p_own = jnp.exp(s_own - m)
    l = jnp.sum(p, axis=1, keepdims=True) + jnp.sum(p_own, axis=1, keepdims=True)
    p_scr[...] = p.astype(BF16)
    acc = _dot(p_own.astype(BF16), jnp.concatenate([vn_ref[...].astype(BF16), pad], axis=0))

    for f in range(n_steps, n_fetch):
        if f + 1 < n_fetch:
            start_fetch(b, f + 1)
        else:
            @pl.when(b + 1 < n_seq)
            def _():
                start_fetch(b + 1, 0)
        wait_fetch(b, f)
        g = f - n_steps
        acc = acc + _dot_t(p_scr[:, g * step_keys:(g + 1) * step_keys], step_pages(f))

    o = acc / l
    out = jnp.zeros((ls, D_ATTN), F32)
    for hd in range(N_HEADS):
        out = out + jnp.where(lane_d // HEAD_DIM == hd, o[hd * ls:(hd + 1) * ls, :], 0.0)
    o_ref[...] = out.astype(o_ref.dtype)


def _attn_sample(q_aug, k_new, v_new, cache_kt, cache_vt, page_table, layer, *, n_seq, seq_len):
    n_pages = page_table.shape[1]
    page_size = cache_kt.shape[3]
    past_len = n_pages * page_size
    assert past_len % MOBA_BLOCK == 0 and MOBA_BLOCK % page_size == 0 and seq_len % 8 == 0
    assert seq_len <= MOBA_BLOCK and past_len // MOBA_BLOCK <= LANES
    pages_per_step = _pick_tile(n_pages, (16, 8, MOBA_BLOCK // page_size))
    n_steps = n_pages // pages_per_step
    rows = N_HEADS * seq_len
    expand = (jnp.arange(past_len)[None, :] // MOBA_BLOCK == jnp.arange(LANES)[:, None]).astype(BF16)

    grid_spec = pltpu.PrefetchScalarGridSpec(
        num_scalar_prefetch=1,
        grid=(n_seq,),
        in_specs=[pl.BlockSpec((seq_len, N_HEADS * SLAB), lambda b, pt: (b, 0)),
                  pl.BlockSpec((seq_len, D_ATTN), lambda b, pt: (b, 0)),
                  pl.BlockSpec((seq_len, D_ATTN), lambda b, pt: (b, 0)),
                  _resident((LANES, past_len), lambda b, pt: (0, 0)),
                  pl.BlockSpec(memory_space=pl.ANY), pl.BlockSpec(memory_space=pl.ANY)],
        out_specs=pl.BlockSpec((seq_len, D_ATTN), lambda b, pt: (b, 0)),
        scratch_shapes=[pltpu.VMEM((2, pages_per_step, D_ATTN, page_size), F32), pltpu.SemaphoreType.DMA((2,)),
                        pltpu.VMEM((rows, past_len), F32), pltpu.VMEM((rows, past_len), BF16)],
    )
    return pl.pallas_call(
        functools.partial(_attn_sample_kernel, layer=layer, pages_per_step=pages_per_step, n_steps=n_steps,
                          page_size=page_size),
        grid_spec=grid_spec,
        out_shape=jax.ShapeDtypeStruct((n_seq * seq_len, D_ATTN), BF16),
        compiler_params=_cparams(("arbitrary",)),
    )(page_table, q_aug, k_new, v_new, expand, cache_kt, cache_vt)


def _transposed_pages(cache):
    depth, n_pool, page_size = cache.shape[:3]
    return jnp.transpose(cache, (0, 1, 3, 4, 2)).reshape(depth, n_pool, D_ATTN, page_size)


def _mix_kernel(x_ref, u_ref, uh_ref, z_ref, zh_ref, a_ref, g_ref,
                cw_ref, cb_ref, lg_ref, lb_ref, wc_ref, pl_ref, ps_ref, wp_ref, wa_ref, wo_ref,
                o_ref, ext_ref, zext_ref, *, prompt, n_pool_prev):
    tl = u_ref.shape[0]
    i = pl.program_id(1)
    d_model = x_ref.shape[1]

    uh, zh = uh_ref[...], zh_ref[...]
    if prompt:
        uh = jnp.where(i == 0, 0.0, uh)
        zh = jnp.where(i == 0, 0.0, zh)
    ext_ref[0:CONV_HALO, :] = uh
    ext_ref[CONV_HALO:CONV_HALO + tl, :] = u_ref[...]
    zext_ref[0:POOL_HALO, :] = zh
    z = z_ref[...]
    zext_ref[POOL_HALO:, :] = z

    ext_ref[CONV_HALO + tl:, :] = jnp.zeros((SUBLANES, D_CONV), F32)
    base = CONV_HALO - (CONV_WIDTH - 1)
    groups = [None] * SUBLANES
    for j in range(CONV_WIDTH):
        k, r = divmod(base + j, SUBLANES)
        term = ext_ref[SUBLANES * k:SUBLANES * k + tl + SUBLANES, :] * cw_ref[j:j + 1, :]
        groups[r] = term if groups[r] is None else groups[r] + term
    y = jnp.zeros((tl, D_CONV), F32) + cb_ref[...]
    for r, grp in enumerate(groups):
        if grp is not None:
            y = y + grp[r:r + tl, :]
    mu = jnp.mean(y, axis=-1, keepdims=True)
    yc = y - mu
    var = jnp.mean(yc * yc, axis=-1, keepdims=True)
    yn = yc * lax.rsqrt(var + EPS) * lg_ref[...] + lb_ref[...]
    c = _dot(jax.nn.silu(yn).astype(BF16), wc_ref[...])

    pos = i * tl + lax.broadcasted_iota(jnp.int32, (tl, 1), 0)
    parts = []
    for gi, w in enumerate(POOL_WINDOWS):
        lo, hi = gi * POOL_GROUP, (gi + 1) * POOL_GROUP
        tot = z[:, lo:hi]
        for back in range(1, w):
            tot = tot + zext_ref[POOL_HALO - back:POOL_HALO - back + tl, lo:hi]
        cnt = jnp.minimum(w, pos + 1 + n_pool_prev).astype(F32)
        d = tot / cnt - z[:, lo:hi]
        parts.append(_dot(d.astype(BF16), pl_ref[gi]))
    pm = jnp.concatenate(parts, axis=1) * ps_ref[...]
    p = _dot(pm.astype(BF16), wp_ref[...])

    a = _dot(a_ref[...], wa_ref[...])
    merged = (g_ref[:, 0:d_model] * a + g_ref[:, d_model:2 * d_model] * c
              + g_ref[:, 2 * d_model:3 * d_model] * p)
    o_ref[...] = x_ref[...] + _dot(merged.astype(BF16), wo_ref[...])


def _mix(x, u, z, attn, gates, conv_hist, pool_hist, lw, *, prompt, tl):
    n_seq, seq_len, d_model = x.shape
    n_tiles = seq_len // tl
    seq = lambda c: pl.BlockSpec((None, tl, c), lambda b, i: (b, i, 0))
    if prompt:
        uh_arr, zh_arr = u, z
        uh_spec = pl.BlockSpec((None, CONV_HALO, D_CONV),
                               lambda b, i: (b, jnp.maximum(i * (tl // CONV_HALO) - 1, 0), 0))
        zh_spec = pl.BlockSpec((None, POOL_HALO, D_POOL),
                               lambda b, i: (b, jnp.maximum(i * (tl // POOL_HALO) - 1, 0), 0))
        n_pool_prev = 0
    else:
        assert n_tiles == 1
        uh_arr = jnp.pad(conv_hist, ((0, 0), (CONV_HALO - conv_hist.shape[1], 0), (0, 0)))
        zh_arr = jnp.pad(pool_hist, ((0, 0), (POOL_HALO - pool_hist.shape[1], 0), (0, 0)))
        uh_spec = pl.BlockSpec((None, CONV_HALO, D_CONV), lambda b, i: (b, 0, 0))
        zh_spec = pl.BlockSpec((None, POOL_HALO, D_POOL), lambda b, i: (b, 0, 0))
        n_pool_prev = POOL_STATE
    const = lambda shape: _resident(shape, lambda b, i: (0,) * len(shape))
    return pl.pallas_call(
        functools.partial(_mix_kernel, prompt=prompt, n_pool_prev=n_pool_prev),
        grid=(n_seq, n_tiles),
        in_specs=[seq(d_model), seq(D_CONV), uh_spec, seq(D_POOL), zh_spec, seq(D_ATTN), seq(N_BRANCHES * d_model),
                  const((CONV_WIDTH, D_CONV)), const((1, D_CONV)), const((1, D_CONV)), const((1, D_CONV)),
                  const((D_CONV, d_model)), const((len(POOL_WINDOWS), POOL_GROUP, POOL_GROUP)),
                  const((1, D_POOL)), const((D_POOL, d_model)), const((D_ATTN, d_model)),
                  const((d_model, d_model))],
        out_specs=seq(d_model),
        out_shape=jax.ShapeDtypeStruct(x.shape, F32),
        scratch_shapes=[pltpu.VMEM((CONV_HALO + tl + SUBLANES, D_CONV), F32),
                        pltpu.VMEM((POOL_HALO + tl, D_POOL), F32)],
        compiler_params=_cparams(("parallel", "arbitrary")),
    )(x, u, uh_arr, z, zh_arr, attn, gates,
      lw["conv_w"], lw["conv_b"], lw["conv_ln_g"], lw["conv_ln_b"], lw["w_conv_out"], lw["pool_lin"],
      lw["pool_scale"], lw["w_pool_out"], lw["w_attn_out"], lw["w_out"])


def _ffn_chunk(d_ff):
    for n in (2, 4, 1):
        if d_ff % (n * LANES) == 0 and d_ff // n <= 2048:
            return d_ff // n
    return d_ff


def _ffn_kernel(x_ref, gain_ref, w1_ref, w3_ref, w2_ref, fg_ref, o_ref, *, chunk, final_norm):
    x = x_ref[...]
    h = _rms(x, gain_ref[...]).astype(BF16)
    acc = x
    for c0 in range(0, w1_ref.shape[1], chunk):
        act = jax.nn.silu(_dot(h, w1_ref[:, c0:c0 + chunk])) * _dot(h, w3_ref[:, c0:c0 + chunk])
        acc = acc + _dot(act.astype(BF16), w2_ref[c0:c0 + chunk, :])
    o_ref[...] = _rms(acc, fg_ref[...]) if final_norm else acc


def _ffn(x2d, gain, w1, w3, w2, final_gain, *, tm):
    t, d_model = x2d.shape
    d_ff = w1.shape[1]
    final_norm = final_gain is not None
    fg = (final_gain if final_norm else gain).reshape(1, d_model)
    row = pl.BlockSpec((tm, d_model), lambda i: (i, 0))
    vec = _resident((1, d_model), lambda i: (0, 0))
    return pl.pallas_call(
        functools.partial(_ffn_kernel, chunk=_ffn_chunk(d_ff), final_norm=final_norm),
        grid=(t // tm,),
        in_specs=[row, vec, _resident((d_model, d_ff), lambda i: (0, 0)), _resident((d_model, d_ff), lambda i: (0, 0)),
                  _resident((d_ff, d_model), lambda i: (0, 0)), vec],
        out_specs=row,
        out_shape=jax.ShapeDtypeStruct((t, d_model), F32),
        compiler_params=_cparams(("parallel",)),
    )(x2d, gain.reshape(1, d_model), w1, w3, w2, fg)


MOE_CHUNK = 256
MOE_TAIL = LANES


def _dot_halves(a_ref, b):
    half = a_ref.shape[0] // 2
    return jnp.concatenate([_dot(a_ref[:half, :], b), _dot(a_ref[half:, :], b)], axis=0)


def _router_kernel(x_ref, gain_ref, r_ref, tri_ref, rank_ref, rankt_ref, combt_ref, cnt_ref):
    tm = x_ref.shape[0]
    lane = lax.broadcasted_iota(jnp.int32, (tm, LANES), 1)
    h32 = _rms(x_ref[...], gain_ref[...])
    logits = jnp.dot(h32, r_ref[...], preferred_element_type=F32, precision=lax.Precision.HIGHEST)
    logits = jnp.where(lane < N_EXPERTS, logits, -jnp.inf)
    m1 = jnp.max(logits, axis=1, keepdims=True)
    i1 = jnp.min(jnp.where(logits == m1, lane, LANES), axis=1, keepdims=True)
    rest = jnp.where(lane == i1, -jnp.inf, logits)
    m2 = jnp.max(rest, axis=1, keepdims=True)
    i2 = jnp.min(jnp.where(rest == m2, lane, LANES), axis=1, keepdims=True)
    e2 = jnp.exp(m2 - m1)
    den = 1.0 + e2
    comb = jnp.where(lane == i1, 1.0 / den, 0.0) + jnp.where(lane == i2, e2 / den, 0.0)
    routed = (lane == i1) | (lane == i2)
    ones = jnp.where(routed, 1.0, 0.0)
    before = _dot(tri_ref[...], ones.astype(BF16))
    rank = jnp.where(routed, before, -1.0)
    rank_ref[...] = rank
    rankt_ref[...] = rank.T[:N_EXPERTS]
    combt_ref[...] = comb.T[:N_EXPERTS]
    cnt_ref[...] = jnp.sum(ones, axis=0, keepdims=True).astype(jnp.int32)


def _moe_kernel(cnt_ref, x_ref, gain_ref, rank_ref, rankt_ref, combt_ref, w1t_ref, w3t_ref, w2t_ref, fg_ref,
                *refs, final_norm, n_carried):
    o_ref, ht_scr, xt_scr, acct_scr, yt_scr = refs[n_carried:]
    i, e, f = pl.program_id(0), pl.program_id(1), pl.program_id(2)
    n_e, n_f = pl.num_programs(1), pl.num_programs(2)
    tm = x_ref.shape[0]
    n_tail = (jnp.maximum(cnt_ref[i * LANES + e] - MOE_CHUNK, 0) + MOE_TAIL - 1) // MOE_TAIL

    @pl.when((e == 0) & (f == 0))
    def _():
        ht_scr[...] = _rms(x_ref[...], gain_ref[...]).T.astype(BF16)
        yt_scr[...] = jnp.zeros_like(yt_scr)

    def expert_pass(base, width):
        cols = pl.ds(base if isinstance(base, int) else pl.multiple_of(base, MOE_TAIL), width)

        @pl.when(f == 0)
        def _():
            lane = lax.broadcasted_iota(jnp.int32, (tm, LANES), 1)
            rank_col = jnp.sum(jnp.where(lane == e, rank_ref[...], 0.0), axis=1, keepdims=True)
            slot = lax.broadcasted_iota(jnp.int32, (tm, width), 1) + base
            take = jnp.where(rank_col == slot.astype(F32), 1.0, 0.0).astype(BF16)
            xt_scr[:, cols] = _dot_halves(ht_scr, take).astype(BF16)

        xt = xt_scr[:, cols]
        act = jax.nn.silu(_dot(w1t_ref[...], xt)) * _dot(w3t_ref[...], xt)
        part = _dot_halves(w2t_ref, act.astype(BF16))

        @pl.when(f == 0)
        def _():
            acct_scr[:, cols] = part

        @pl.when(f > 0)
        def _():
            acct_scr[:, cols] += part

    expert_pass(0, MOE_CHUNK)

    def tail_pass(c, carry):
        expert_pass(MOE_CHUNK + c * MOE_TAIL, MOE_TAIL)
        return carry

    lax.fori_loop(0, n_tail, tail_pass, 0)

    @pl.when(f == n_f - 1)
    def _():
        rank_row = rankt_ref[pl.ds(e, 1), :]
        comb_row = combt_ref[pl.ds(e, 1), :]

        def scatter_pass(base, width):
            cols = pl.ds(base if isinstance(base, int) else pl.multiple_of(base, MOE_TAIL), width)
            slot = lax.broadcasted_iota(jnp.int32, (width, tm), 0) + base
            put = jnp.where(rank_row == slot.astype(F32), 1.0, 0.0).astype(BF16)
            a = acct_scr[:, cols]
            hi = a.astype(BF16)
            lo = (a - hi.astype(F32)).astype(BF16)
            yt_scr[...] += comb_row * (_dot(hi, put) + _dot(lo, put))

        scatter_pass(0, MOE_CHUNK)

        def tail_scatter(c, carry):
            scatter_pass(MOE_CHUNK + c * MOE_TAIL, MOE_TAIL)
            return carry

        lax.fori_loop(0, n_tail, tail_scatter, 0)

    @pl.when((e == n_e - 1) & (f == n_f - 1))
    def _():
        out = x_ref[...] + yt_scr[...].T
        o_ref[...] = _rms(out, fg_ref[...]) if final_norm else out


def _moe_tiling(t):
    main, rest = 7 * LANES, MOE_CHUNK
    if t > rest and (t - rest) % main == 0:
        return [(main, 0, (t - rest) // main), (rest, (t - rest) // rest, 1)]
    tm = _pick_tile(t, (1024, 512, MOE_CHUNK))
    return [(tm, 0, t // tm)]


def _moe(x2d, gain, router, w1t, w3t, w2t, final_gain, *, tm, tile0, n_tiles, out_carried=None):
    t, d_model = x2d.shape
    n_e, d_ff, _ = w1t.shape
    assert n_e == N_EXPERTS and tm % LANES == 0 and tm >= MOE_CHUNK
    chunk = _ffn_chunk(d_ff)
    n_f = d_ff // chunk
    final_norm = final_gain is not None
    fg = (final_gain if final_norm else gain).reshape(1, d_model)
    gain2 = gain.reshape(1, d_model)
    r_pad = jnp.pad(router, ((0, 0), (0, LANES - n_e)))
    earlier = (jnp.arange(tm)[None, :] < jnp.arange(tm)[:, None]).astype(BF16)

    rank, rank_t, comb_t, counts = pl.pallas_call(
        _router_kernel,
        grid=(n_tiles,),
        in_specs=[pl.BlockSpec((tm, d_model), lambda i: (i + tile0, 0)), _resident((1, d_model), lambda i: (0, 0)),
                  _resident((d_model, LANES), lambda i: (0, 0)), _resident((tm, tm), lambda i: (0, 0))],
        out_specs=[pl.BlockSpec((tm, LANES), lambda i: (i, 0)),
                   pl.BlockSpec((None, N_EXPERTS, tm), lambda i: (i, 0, 0)),
                   pl.BlockSpec((None, N_EXPERTS, tm), lambda i: (i, 0, 0)),
                   pl.BlockSpec((None, 1, LANES), lambda i: (i, 0, 0))],
        out_shape=[jax.ShapeDtypeStruct((n_tiles * tm, LANES), F32),
                   jax.ShapeDtypeStruct((n_tiles, N_EXPERTS, tm), F32),
                   jax.ShapeDtypeStruct((n_tiles, N_EXPERTS, tm), F32),
                   jax.ShapeDtypeStruct((n_tiles, 1, LANES), jnp.int32)],
        compiler_params=_cparams(("parallel",)),
    )(x2d, gain2, r_pad, earlier)

    row = pl.BlockSpec((tm, d_model), lambda i, e, f, cnt: (i + tile0, 0))
    vec = _resident((1, d_model), lambda i, e, f, cnt: (0, 0))
    tile_t = pl.BlockSpec((None, N_EXPERTS, tm), lambda i, e, f, cnt: (i, 0, 0))
    carried = () if out_carried is None else (out_carried,)
    grid_spec = pltpu.PrefetchScalarGridSpec(
        num_scalar_prefetch=1,
        grid=(n_tiles, n_e, n_f),
        in_specs=[row, vec, pl.BlockSpec((tm, LANES), lambda i, e, f, cnt: (i, 0)), tile_t, tile_t,
                  pl.BlockSpec((None, chunk, d_model), lambda i, e, f, cnt: (e, f, 0)),
                  pl.BlockSpec((None, chunk, d_model), lambda i, e, f, cnt: (e, f, 0)),
                  pl.BlockSpec((None, d_model, chunk), lambda i, e, f, cnt: (e, 0, f)), vec]
                 + [pl.BlockSpec(memory_space=pl.ANY)] * len(carried),
        out_specs=row,
        scratch_shapes=[pltpu.VMEM((d_model, tm), BF16), pltpu.VMEM((d_model, tm), BF16),
                        pltpu.VMEM((d_model, tm), F32), pltpu.VMEM((d_model, tm), F32)],
    )
    operands = (counts.reshape(n_tiles * LANES), x2d, gain2, rank, rank_t, comb_t, w1t, w3t, w2t, fg) + carried
    return pl.pallas_call(
        functools.partial(_moe_kernel, final_norm=final_norm, n_carried=len(carried)),
        grid_spec=grid_spec,
        out_shape=jax.ShapeDtypeStruct((t, d_model), F32),
        input_output_aliases={len(operands) - 1: 0} if carried else {},
        compiler_params=_cparams(("parallel", "arbitrary", "arbitrary")),
    )(*operands)


def _pick_tile(n, prefs):
    for p in prefs:
        if n % p == 0:
            return p
    return n


def _forward(x, pos, cache, layers, norm_final, *, prompt):
    n_seq, seq_len, d_model = x.shape
    t = n_seq * seq_len
    depth = len(layers)
    if prompt:
        assert seq_len % MOBA_BLOCK == 0 and seq_len // MOBA_BLOCK <= HEAD_DIM
        tm_in = _pick_tile(seq_len, (512, MOBA_BLOCK))
        tables = _rope_tables(pos)
        tl = _pick_tile(seq_len, (512, 256, 128, 64, 32))
        tm_ffn = _pick_tile(t, (512, 256))
        tm_moe = _pick_tile(t, (1024, 512, 256))
    else:
        tm_in = t
        tables = tuple(jnp.tile(tb, (n_seq, 1)) for tb in _rope_tables(pos))
        tl = seq_len
        tm_ffn = tm_moe = t
        cache_kt, cache_vt, page_table, state_conv, state_pool = cache
    ks, vs, cs, ps = [], [], [], []
    kv_t = None
    for l, lw in enumerate(layers):
        outs = _inproj(x.reshape(t, d_model), lw["norm_mix"], lw["w_in_aug"], tables,
                       prompt=prompt, seq_len=seq_len, tm=tm_in, layer=l, depth=depth, kv_carried=kv_t)
        if prompt:
            q_aug, k_aug, kt_all, vt_all, v_bf, kmean, u, z, gates = outs
            kv_t = (kt_all, vt_all)
            attn = _attn_prompt(q_aug, k_aug, v_bf, kmean, n_seq=n_seq, seq_len=seq_len, tq=2 * MOBA_BLOCK)
            conv_hist = pool_hist = None
        else:
            q_aug, k_new, v_new, u, z, gates = outs
            attn = _attn_sample(q_aug, k_new, v_new, cache_kt, cache_vt, page_table, l,
                                n_seq=n_seq, seq_len=seq_len)
            conv_hist, pool_hist = state_conv[l], state_pool[l]
        r3 = lambda a: a.reshape(n_seq, seq_len, a.shape[-1])
        u3, z3 = r3(u), r3(z)
        x = _mix(x, u3, z3, r3(attn), r3(gates), conv_hist, pool_hist, lw, prompt=prompt, tl=tl)
        final_gain = norm_final if l == depth - 1 else None
        x2d = x.reshape(t, d_model)
        if lw["kind"] == "dense":
            x2d = _ffn(x2d, lw["norm_ffn"], lw["w1"], lw["w3"], lw["w2"], final_gain, tm=tm_ffn)
        else:
            out = None
            for tm_moe, tile0, n_tiles in _moe_tiling(t):
                out = _moe(x2d, lw["norm_ffn"], lw["router"], lw["w1"], lw["w3"], lw["w2"], final_gain,
                           tm=tm_moe, tile0=tile0, n_tiles=n_tiles, out_carried=out)
            x2d = out
        x = x2d.reshape(n_seq, seq_len, d_model)
        if prompt:
            cs.append(u3[:, seq_len - (CONV_WIDTH - 1):])
            ps.append(z3[:, seq_len - POOL_STATE:])
        else:
            ks.append(k_new.reshape(n_seq, seq_len, N_HEADS, HEAD_DIM))
            vs.append(v_new.reshape(n_seq, seq_len, N_HEADS, HEAD_DIM))
            cs.append(jnp.concatenate([conv_hist, u3], axis=1)[:, -(CONV_WIDTH - 1):])
            ps.append(jnp.concatenate([pool_hist, z3], axis=1)[:, -POOL_STATE:])
    if prompt:
        untranspose = lambda a: a.reshape(depth, n_seq, N_HEADS, HEAD_DIM, seq_len).transpose(0, 1, 4, 2, 3)
        k_out, v_out = untranspose(kv_t[0]), untranspose(kv_t[1])
    else:
        k_out, v_out = jnp.stack(ks), jnp.stack(vs)
    return x, k_out, v_out, jnp.stack(cs), jnp.stack(ps)


def kernel(x_prompt, x_sample, cache_k, cache_v, page_table, state_conv, state_pool, norm_mix, norm_ffn, norm_final, w_in, w_attn_out, conv_w, conv_b, conv_ln_g, conv_ln_b, w_conv_out, pool_lin, pool_scale, w_pool_out, w_out, ffn_w1, ffn_w3, ffn_w2, moe_router, moe_w1, moe_w3, moe_w2):
    depth = norm_mix.shape[0]
    layers = []
    for l in range(depth):
        lw = dict(
            norm_mix=norm_mix[l], norm_ffn=norm_ffn[l],
            w_in_aug=_prep_w_in(w_in[l]),
            w_attn_out=w_attn_out[l].astype(BF16), conv_w=conv_w[l], conv_b=conv_b[l].reshape(1, -1),
            conv_ln_g=conv_ln_g[l].reshape(1, -1), conv_ln_b=conv_ln_b[l].reshape(1, -1),
            w_conv_out=w_conv_out[l].astype(BF16), pool_lin=pool_lin[l].astype(BF16),
            pool_scale=pool_scale[l].reshape(1, -1), w_pool_out=w_pool_out[l].astype(BF16),
            w_out=w_out[l].astype(BF16))
        i = l // 2
        if l % 2 == 0:
            lw.update(kind="dense", w1=ffn_w1[i].astype(BF16), w3=ffn_w3[i].astype(BF16), w2=ffn_w2[i].astype(BF16))
        else:
            tr = lambda w: jnp.swapaxes(w, 1, 2).astype(BF16)
            lw.update(kind="moe", router=moe_router[i], w1=tr(moe_w1[i]), w3=tr(moe_w3[i]), w2=tr(moe_w2[i]))
        layers.append(lw)

    past_len = page_table.shape[1] * cache_k.shape[2]
    pos_prompt = jnp.arange(x_prompt.shape[1], dtype=jnp.int32)
    pos_sample = past_len + jnp.arange(x_sample.shape[1], dtype=jnp.int32)
    y_p, k_p, v_p, conv_p, pool_p = _forward(x_prompt, pos_prompt, None, layers, norm_final, prompt=True)
    cache = (_transposed_pages(cache_k), _transposed_pages(cache_v), page_table, state_conv, state_pool)
    y_s, k_s, v_s, conv_s, pool_s = _forward(x_sample, pos_sample, cache, layers, norm_final, prompt=False)
    return (y_p, y_s, k_p, v_p, k_s, v_s, conv_p, conv_s, pool_p, pool_s)
```

```python
import functools
import math

import jax
import jax.numpy as jnp
import numpy as np
from jax import lax
from jax.experimental import pallas as pl
from jax.experimental.pallas import tpu as pltpu

F32 = jnp.float32
BF16 = jnp.bfloat16

N_HEADS = 8
HEAD_DIM = 64
D_ATTN = N_HEADS * HEAD_DIM
ROPE_DIM = HEAD_DIM // 4
ROPE_THETA = 500000.0
MOBA_BLOCK = 256
MOBA_TOPK = 3
D_CONV = 512
CONV_WIDTH = 31
D_POOL = 512
POOL_WINDOWS = (2, 4, 8, 16)
POOL_GROUP = D_POOL // len(POOL_WINDOWS)
POOL_STATE = max(POOL_WINDOWS) - 1
N_BRANCHES = 3
N_EXPERTS = 8
EPS = 1e-6

LANES = 128
SUBLANES = 8
SLAB = 2 * HEAD_DIM
CONV_HALO = 32
POOL_HALO = 16
NEG = -1e30
VMEM_LIMIT = 56 * 1024 * 1024


def _cparams(sem):
    return pltpu.CompilerParams(dimension_semantics=sem, vmem_limit_bytes=VMEM_LIMIT)


def _resident(shape, index_map):
    return pl.BlockSpec(shape, index_map, pipeline_mode=pl.Buffered(1))


def _rms(x, gain):
    ms = jnp.mean(x * x, axis=-1, keepdims=True)
    return x * lax.rsqrt(ms + EPS) * gain


def _dot(a, b):
    return jnp.dot(a, b, preferred_element_type=F32)


def _dot_t(a, b):
    return lax.dot_general(a, b, (((1,), (1,)), ((), ())), preferred_element_type=F32)


_C_Q = 0
_C_K = _C_Q + D_ATTN
_C_V = _C_K + D_ATTN
_C_GLU = _C_V + D_ATTN
_C_PZ = _C_GLU + 2 * D_CONV
_C_G = _C_PZ + D_POOL


def _prep_w_in(w_in):
    scale = jnp.where(jnp.arange(w_in.shape[1]) < D_ATTN, HEAD_DIM ** -0.5, 1.0)
    return (w_in * scale[None, :]).astype(BF16)


def _rope_tables(pos):
    half = ROPE_DIM // 2
    step = -2.0 * math.log(ROPE_THETA) / ROPE_DIM
    step_hi = float(np.float32(step))
    step_lo = float(np.float32(step - step_hi))
    idx = jnp.arange(half, dtype=F32)
    inv_freq = jnp.exp(idx * step_hi + idx * step_lo)
    ang = pos.astype(F32)[:, None] * inv_freq[None, :]
    cos, sin = jnp.cos(ang), jnp.sin(ang)
    n = pos.shape[0]
    ones = jnp.ones((n, HEAD_DIM - ROPE_DIM), F32)
    zeros8 = jnp.zeros((n, half), F32)
    zeros48 = jnp.zeros((n, HEAD_DIM - ROPE_DIM), F32)
    c = jnp.concatenate([cos, cos, ones], axis=1)
    s_up = jnp.concatenate([zeros8, sin, zeros48], axis=1)
    s_dn = jnp.concatenate([-sin, zeros8, zeros48], axis=1)
    tile2 = lambda t: jnp.concatenate([t, t], axis=1)
    return tile2(c), tile2(s_up), tile2(s_dn)


def _inproj_kernel(x_ref, gain_ref, w_ref, cos_ref, sup_ref, sdn_ref, *refs, prompt, tiles_per_seq, n_carried):
    out_refs = refs[n_carried:]
    if prompt:
        qa_ref, ka_ref, k_ref, v_ref, vb_ref, km_ref, u_ref, pz_ref, g_ref = out_refs
    else:
        qa_ref, k_ref, v_ref, u_ref, pz_ref, g_ref = out_refs
    tm = x_ref.shape[0]
    h = _rms(x_ref[...], gain_ref[...]).astype(BF16)
    cosv, sup, sdn = cos_ref[...], sup_ref[...], sdn_ref[...]

    def rope(slab):
        return slab * cosv + pltpu.roll(slab, ROPE_DIM // 2, 1) * sup + pltpu.roll(slab, LANES - ROPE_DIM // 2, 1) * sdn

    lane = lax.broadcasted_iota(jnp.int32, (tm, LANES), 1)
    first = lane < HEAD_DIM
    q_all = _dot(h, w_ref[:, _C_Q:_C_K])
    for pair in range(N_HEADS // 2):
        q_nat = rope(q_all[:, pair * LANES:(pair + 1) * LANES])
        qa_ref[:, (2 * pair) * SLAB:(2 * pair + 1) * SLAB] = jnp.where(first, q_nat, 0.0).astype(BF16)
        qa_ref[:, (2 * pair + 1) * SLAB:(2 * pair + 2) * SLAB] = jnp.where(first, 0.0, q_nat).astype(BF16)

    k_all = _dot(h, w_ref[:, _C_K:_C_V])
    if prompt:
        row = lax.broadcasted_iota(jnp.int32, (tm, LANES), 0)
        blk = ((pl.program_id(0) % tiles_per_seq) * tm + row) // MOBA_BLOCK
        onehot_even = (lane == blk + HEAD_DIM).astype(F32)
        onehot_odd = (lane == blk).astype(F32)
    for pair in range(N_HEADS // 2):
        k_nat = rope(k_all[:, pair * LANES:(pair + 1) * LANES])
        if not prompt:
            k_ref[:, pair * LANES:(pair + 1) * LANES] = k_nat
        else:
            k_ref[pair * LANES:(pair + 1) * LANES, :] = k_nat.T
            k_even, k_odd = jnp.where(first, k_nat, 0.0), jnp.where(first, 0.0, k_nat)
            for hd, k_slab, onehot in ((2 * pair, k_even, onehot_even), (2 * pair + 1, k_odd, onehot_odd)):
                ka_ref[:, hd * SLAB:(hd + 1) * SLAB] = (k_slab + onehot).astype(BF16)
                km = jnp.mean(k_slab.reshape(tm // MOBA_BLOCK, MOBA_BLOCK, SLAB), axis=1)
                km_ref[:, 0, hd * SLAB:(hd + 1) * SLAB] = km

    v_all = _dot(h, w_ref[:, _C_V:_C_GLU])
    if prompt:
        vb_ref[...] = v_all.astype(BF16)
        for c in range(D_ATTN // LANES):
            v_ref[c * LANES:(c + 1) * LANES, :] = v_all[:, c * LANES:(c + 1) * LANES].T
    else:
        v_ref[...] = v_all

    glu = _dot(h, w_ref[:, _C_GLU:_C_PZ])
    u_ref[...] = glu[:, :D_CONV] * jax.nn.sigmoid(glu[:, D_CONV:])
    pz_ref[...] = _dot(h, w_ref[:, _C_PZ:_C_G])
    g_ref[...] = jax.nn.sigmoid(_dot(h, w_ref[:, _C_G:]))


def _inproj(x2d, gain, w_aug, tables, *, prompt, seq_len, tm, layer=0, depth=1, kv_carried=None):
    t, d_model = x2d.shape
    n_tiles = t // tm
    cos_t, sup_t, sdn_t = tables
    n_tab = cos_t.shape[0] // tm
    tiles_per_seq = max(seq_len // tm, 1)
    n_cols = w_aug.shape[1]
    row = lambda c: pl.BlockSpec((tm, c), lambda i: (i, 0))
    tab = pl.BlockSpec((tm, LANES), lambda i: (i % n_tab, 0))
    f = jax.ShapeDtypeStruct
    outs = [(f((t, N_HEADS * SLAB), BF16), row(N_HEADS * SLAB))]
    if prompt:
        outs.append((f((t, N_HEADS * SLAB), BF16), row(N_HEADS * SLAB)))
        kv_t = (f((depth, t // seq_len, D_ATTN, seq_len), F32),
                pl.BlockSpec((None, None, D_ATTN, tm), lambda i: (layer, i // tiles_per_seq, 0, i % tiles_per_seq)))
        outs += [kv_t, kv_t]
    else:
        outs += [(f((t, D_ATTN), F32), row(D_ATTN)), (f((t, D_ATTN), F32), row(D_ATTN))]
    if prompt:
        bpt = tm // MOBA_BLOCK
        outs.append((f((t, D_ATTN), BF16), row(D_ATTN)))
        outs.append((f((t // MOBA_BLOCK, 1, N_HEADS * SLAB), F32),
                     pl.BlockSpec((bpt, 1, N_HEADS * SLAB), lambda i: (i, 0, 0))))
    outs += [(f((t, D_CONV), F32), row(D_CONV)), (f((t, D_POOL), F32), row(D_POOL)),
             (f((t, N_BRANCHES * d_model), F32), row(N_BRANCHES * d_model))]
    carried = tuple(kv_carried) if kv_carried is not None else ()
    n_in = 6
    return pl.pallas_call(
        functools.partial(_inproj_kernel, prompt=prompt, tiles_per_seq=tiles_per_seq, n_carried=len(carried)),
        grid=(n_tiles,),
        in_specs=[row(d_model), _resident((1, d_model), lambda i: (0, 0)),
                  _resident((d_model, n_cols), lambda i: (0, 0)), tab, tab, tab]
                 + [pl.BlockSpec(memory_space=pl.ANY)] * len(carried),
        out_specs=[o[1] for o in outs],
        out_shape=[o[0] for o in outs],
        input_output_aliases={n_in + j: 2 + j for j in range(len(carried))},
        compiler_params=_cparams(("parallel",)),
    )(x2d, gain.reshape(1, d_model), w_aug, cos_t, sup_t, sdn_t, *carried)


def _top_blocks(gate, valid, idx, axis):
    g = jnp.where(valid, gate, -jnp.inf)
    picked = jnp.zeros(gate.shape, jnp.bool_)
    for _ in range(MOBA_TOPK):
        m = jnp.max(g, axis=axis, keepdims=True)
        first = jnp.min(jnp.where(g == m, idx, jnp.int32(1 << 20)), axis=axis, keepdims=True)
        pick = (idx == first) & (m > -jnp.inf)
        picked = picked | pick
        g = jnp.where(pick, -jnp.inf, g)
    return picked


def _attn_prompt_kernel(q_ref, k_ref, v_ref, km_ref, o_ref):
    ti = pl.program_id(1)
    tq = q_ref.shape[0]
    nb = km_ref.shape[0]
    span = 2 * MOBA_BLOCK
    lane = lax.broadcasted_iota(jnp.int32, (tq, LANES), 1)
    blk_row = lax.broadcasted_iota(jnp.int32, (nb, tq), 0)
    own_t = ti * (tq // MOBA_BLOCK) + lax.broadcasted_iota(jnp.int32, (nb, tq), 1) // MOBA_BLOCK
    row = lax.broadcasted_iota(jnp.int32, (tq, span), 0)
    col = lax.broadcasted_iota(jnp.int32, (tq, span), 1)

    q_full = []
    for hd in range(N_HEADS):
        qs = q_ref[:, hd * SLAB:(hd + 1) * SLAB]
        gate_t = _dot_t(km_ref[:, hd * SLAB:(hd + 1) * SLAB].astype(BF16), qs)
        picked = _top_blocks(gate_t, blk_row < own_t, blk_row, 0)
        pen_t = jnp.where(picked | (blk_row == own_t), 0.0, NEG)
        pieces = [pen_t]
        if nb < HEAD_DIM:
            pieces.append(jnp.full((HEAD_DIM - nb, tq), NEG, F32))
        zeros = jnp.zeros((HEAD_DIM, tq), F32)
        pieces = [zeros] + pieces if hd % 2 == 0 else pieces + [zeros]
        pen = jnp.concatenate(pieces, axis=0).T
        q_full.append((qs.astype(F32) + pen).astype(BF16))

    def sweep(start, carry, diag, width=span):
        start = start if isinstance(start, int) else pl.multiple_of(start, span)
        new = []
        for pair in range(N_HEADS // 2):
            vb = v_ref[pl.ds(start, width), pair * LANES:(pair + 1) * LANES]
            upd = []
            for hd in (2 * pair, 2 * pair + 1):
                s = _dot_t(q_full[hd], k_ref[pl.ds(start, width), hd * SLAB:(hd + 1) * SLAB])
                if diag is not None:
                    s = jnp.where(col + diag * span <= row, s, NEG)
                s_max = jnp.max(s, axis=1, keepdims=True)
                if carry is None:
                    m_new, alpha = s_max, None
                else:
                    m_old = carry[3 * pair + (hd % 2)]
                    m_new = jnp.maximum(m_old, s_max)
                    alpha = jnp.exp(m_old - m_new)
                upd.append((m_new, alpha, jnp.exp(s - m_new).astype(BF16)))
            v_ext = jnp.concatenate([vb, jnp.ones((width, LANES), BF16)], axis=1)
            pv2 = _dot(jnp.concatenate([upd[0][2], upd[1][2]], axis=0), v_ext)
            pv = jnp.where(lane < HEAD_DIM, pv2[:tq, :LANES], pv2[tq:, :LANES])
            l0, l1 = pv2[:tq, LANES:], pv2[tq:, LANES:]
            if carry is None:
                acc = pv
            else:
                l0 = upd[0][1] * carry[3 * pair + 2][0] + l0
                l1 = upd[1][1] * carry[3 * pair + 2][1] + l1
                acc = jnp.where(lane < HEAD_DIM, upd[0][1], upd[1][1]) * carry[3 * pair + 2][2] + pv
            new += [upd[0][0], upd[1][0], (l0, l1, acc)]
        return tuple(new)

    state = None
    for d in range(tq // span):
        state = sweep(ti * tq + d * span, state, d)
    n_past = ti * (tq // span)
    odd = n_past % 2
    state = lax.fori_loop(0, odd, lambda i, c: sweep(0, c, None), state)
    state = lax.fori_loop(0, n_past // 2, lambda i, c: sweep((odd + 2 * i) * span, c, None, 2 * span), state)
    for pair in range(N_HEADS // 2):
        l0, l1, acc = state[3 * pair + 2]
        o_ref[:, pair * LANES:(pair + 1) * LANES] = (acc / jnp.where(lane < HEAD_DIM, l0, l1)).astype(o_ref.dtype)


def _attn_prompt(q_aug, k_aug, v_bf, kmean, *, n_seq, seq_len, tq):
    t = q_aug.shape[0]
    nb = seq_len // MOBA_BLOCK
    n_tiles = seq_len // tq
    assert tq % (2 * MOBA_BLOCK) == 0 and seq_len % tq == 0
    km = kmean.reshape(n_seq, nb, N_HEADS * SLAB)
    return pl.pallas_call(
        _attn_prompt_kernel,
        grid=(n_seq, n_tiles),
        in_specs=[
            pl.BlockSpec((tq, N_HEADS * SLAB), lambda b, i: (b * n_tiles + i, 0)),
            _resident((seq_len, N_HEADS * SLAB), lambda b, i: (b, 0)),
            _resident((seq_len, D_ATTN), lambda b, i: (b, 0)),
            pl.BlockSpec((None, nb, N_HEADS * SLAB), lambda b, i: (b, 0, 0)),
        ],
        out_specs=pl.BlockSpec((tq, D_ATTN), lambda b, i: (b * n_tiles + i, 0)),
        out_shape=jax.ShapeDtypeStruct((t, D_ATTN), BF16),
        compiler_params=_cparams(("parallel", "arbitrary")),
    )(q_aug, k_aug, v_bf, km)


def _attn_sample_kernel(pt_ref, q_ref, kn_ref, vn_ref, e_ref, kc_ref, vc_ref, o_ref, buf, sem, s_scr, p_scr,
                        *, layer, pages_per_step, n_steps, page_size):
    b = pl.program_id(0)
    n_seq = pl.num_programs(0)
    ls = q_ref.shape[0]
    rows = N_HEADS * ls
    step_keys = pages_per_step * page_size
    pages_per_block = MOBA_BLOCK // page_size
    blocks_per_step = pages_per_step // pages_per_block
    n_fetch = 2 * n_steps
    lane_d = lax.broadcasted_iota(jnp.int32, (ls, D_ATTN), 1)
    lane = lax.broadcasted_iota(jnp.int32, (rows, LANES), 1)

    def page_copy(seq, f, i):
        cache = kc_ref if f < n_steps else vc_ref
        page = pt_ref[seq, (f % n_steps) * pages_per_step + i]
        return pltpu.make_async_copy(cache.at[layer, page], buf.at[f % 2, i], sem.at[f % 2])

    def start_fetch(seq, f):
        for i in range(pages_per_step):
            page_copy(seq, f, i).start()

    def wait_fetch(seq, f):
        for i in range(pages_per_step):
            page_copy(seq, f, i).wait()

    def step_pages(f):
        return jnp.concatenate([buf[f % 2, i].astype(BF16) for i in range(pages_per_step)], axis=1)

    @pl.when(b == 0)
    def _():
        start_fetch(b, 0)

    qa = q_ref[...]
    lane_q = lax.broadcasted_iota(jnp.int32, (ls, LANES), 1)
    q_nat = jnp.concatenate(
        [jnp.where(lane_q < HEAD_DIM, qa[:, (2 * p) * SLAB:(2 * p + 1) * SLAB],
                   qa[:, (2 * p + 1) * SLAB:(2 * p + 2) * SLAB]) for p in range(N_HEADS // 2)], axis=1)
    qm = jnp.concatenate(
        [jnp.where(lane_d // HEAD_DIM == hd, q_nat, jnp.zeros_like(q_nat)) for hd in range(N_HEADS)], axis=0)

    gate = jnp.zeros((rows, LANES), F32)
    for f in range(n_steps):
        start_fetch(b, f + 1)
        wait_fetch(b, f)
        s = _dot(qm, step_pages(f))
        s_scr[:, f * step_keys:(f + 1) * step_keys] = s
        for blk in range(blocks_per_step):
            tot = jnp.sum(s[:, blk * MOBA_BLOCK:(blk + 1) * MOBA_BLOCK], axis=1, keepdims=True)
            gate = gate + jnp.where(lane == f * blocks_per_step + blk, tot, 0.0)

    picked = _top_blocks(gate * (1.0 / MOBA_BLOCK), lane < n_steps * blocks_per_step, lane, 1)
    sel = _dot(jnp.where(picked, 1.0, 0.0).astype(BF16), e_ref[...])
    s = jnp.where(sel > 0.5, s_scr[...], NEG)
    pad = jnp.zeros((LANES - ls, D_ATTN), BF16)
    s_own = _dot_t(qm, jnp.concatenate([kn_ref[...].astype(BF16), pad], axis=0))
    qi = lax.broadcasted_iota(jnp.int32, (rows, LANES), 0) % ls
    s_own = jnp.where(lane <= qi, s_own, NEG)
    m = jnp.maximum(jnp.max(s, axis=1, keepdims=True), jnp.max(s_own, axis=1, keepdims=True))
    p = jnp.exp(s - m)
    p_own = jnp.exp(s_own - m)
    l = jnp.sum(p, axis=1, keepdims=True) + jnp.sum(p_own, axis=1, keepdims=True)
    p_scr[...] = p.astype(BF16)
    acc = _dot(p_own.astype(BF16), jnp.concatenate([vn_ref[...].astype(BF16), pad], axis=0))

    for f in range(n_steps, n_fetch):
        if f + 1 < n_fetch:
            start_fetch(b, f + 1)
        else:
            @pl.when(b + 1 < n_seq)
            def _():
                start_fetch(b + 1, 0)
        wait_fetch(b, f)
        g = f - n_steps
        acc = acc + _dot_t(p_scr[:, g * step_keys:(g + 1) * step_keys], step_pages(f))

    o = acc / l
    out = jnp.zeros((ls, D_ATTN), F32)
    for hd in range(N_HEADS):
        out = out + jnp.where(lane_d // HEAD_DIM == hd, o[hd * ls:(hd + 1) * ls, :], 0.0)
    o_ref[...] = out.astype(o_ref.dtype)


def _attn_sample(q_aug, k_new, v_new, cache_kt, cache_vt, page_table, layer, *, n_seq, seq_len):
    n_pages = page_table.shape[1]
    page_size = cache_kt.shape[3]
    past_len = n_pages * page_size
    assert past_len % MOBA_BLOCK == 0 and MOBA_BLOCK % page_size == 0 and seq_len % 8 == 0
    assert seq_len <= MOBA_BLOCK and past_len // MOBA_BLOCK <= LANES
    pages_per_step = _pick_tile(n_pages, (16, 8, MOBA_BLOCK // page_size))
    n_steps = n_pages // pages_per_step
    rows = N_HEADS * seq_len
    expand = (jnp.arange(past_len)[None, :] // MOBA_BLOCK == jnp.arange(LANES)[:, None]).astype(BF16)

    grid_spec = pltpu.PrefetchScalarGridSpec(
        num_scalar_prefetch=1,
        grid=(n_seq,),
        in_specs=[pl.BlockSpec((seq_len, N_HEADS * SLAB), lambda b, pt: (b, 0)),
                  pl.BlockSpec((seq_len, D_ATTN), lambda b, pt: (b, 0)),
                  pl.BlockSpec((seq_len, D_ATTN), lambda b, pt: (b, 0)),
                  _resident((LANES, past_len), lambda b, pt: (0, 0)),
                  pl.BlockSpec(memory_space=pl.ANY), pl.BlockSpec(memory_space=pl.ANY)],
        out_specs=pl.BlockSpec((seq_len, D_ATTN), lambda b, pt: (b, 0)),
        scratch_shapes=[pltpu.VMEM((2, pages_per_step, D_ATTN, page_size), F32), pltpu.SemaphoreType.DMA((2,)),
                        pltpu.VMEM((rows, past_len), F32), pltpu.VMEM((rows, past_len), BF16)],
    )
    return pl.pallas_call(
        functools.partial(_attn_sample_kernel, layer=layer, pages_per_step=pages_per_step, n_steps=n_steps,
                          page_size=page_size),
        grid_spec=grid_spec,
        out_shape=jax.ShapeDtypeStruct((n_seq * seq_len, D_ATTN), BF16),
        compiler_params=_cparams(("arbitrary",)),
    )(page_table, q_aug, k_new, v_new, expand, cache_kt, cache_vt)


def _transposed_pages(cache):
    depth, n_pool, page_size = cache.shape[:3]
    return jnp.transpose(cache, (0, 1, 3, 4, 2)).reshape(depth, n_pool, D_ATTN, page_size)


def _mix_kernel(x_ref, u_ref, uh_ref, z_ref, zh_ref, a_ref, g_ref,
                cw_ref, cb_ref, lg_ref, lb_ref, wc_ref, pl_ref, ps_ref, wp_ref, wa_ref, wo_ref,
                o_ref, ext_ref, zext_ref, *, prompt, n_pool_prev):
    tl = u_ref.shape[0]
    i = pl.program_id(1)
    d_model = x_ref.shape[1]

    uh, zh = uh_ref[...], zh_ref[...]
    if prompt:
        uh = jnp.where(i == 0, 0.0, uh)
        zh = jnp.where(i == 0, 0.0, zh)
    ext_ref[0:CONV_HALO, :] = uh
    ext_ref[CONV_HALO:CONV_HALO + tl, :] = u_ref[...]
    zext_ref[0:POOL_HALO, :] = zh
    z = z_ref[...]
    zext_ref[POOL_HALO:, :] = z

    ext_ref[CONV_HALO + tl:, :] = jnp.zeros((SUBLANES, D_CONV), F32)
    base = CONV_HALO - (CONV_WIDTH - 1)
    groups = [None] * SUBLANES
    for j in range(CONV_WIDTH):
        k, r = divmod(base + j, SUBLANES)
        term = ext_ref[SUBLANES * k:SUBLANES * k + tl + SUBLANES, :] * cw_ref[j:j + 1, :]
        groups[r] = term if groups[r] is None else groups[r] + term
    y = jnp.zeros((tl, D_CONV), F32) + cb_ref[...]
    for r, grp in enumerate(groups):
        if grp is not None:
            y = y + grp[r:r + tl, :]
    mu = jnp.mean(y, axis=-1, keepdims=True)
    yc = y - mu
    var = jnp.mean(yc * yc, axis=-1, keepdims=True)
    yn = yc * lax.rsqrt(var + EPS) * lg_ref[...] + lb_ref[...]
    c = _dot(jax.nn.silu(yn).astype(BF16), wc_ref[...])

    pos = i * tl + lax.broadcasted_iota(jnp.int32, (tl, 1), 0)
    parts = []
    for gi, w in enumerate(POOL_WINDOWS):
        lo, hi = gi * POOL_GROUP, (gi + 1) * POOL_GROUP
        tot = z[:, lo:hi]
        for back in range(1, w):
            tot = tot + zext_ref[POOL_HALO - back:POOL_HALO - back + tl, lo:hi]
        cnt = jnp.minimum(w, pos + 1 + n_pool_prev).astype(F32)
        d = tot / cnt - z[:, lo:hi]
        parts.append(_dot(d.astype(BF16), pl_ref[gi]))
    pm = jnp.concatenate(parts, axis=1) * ps_ref[...]
    p = _dot(pm.astype(BF16), wp_ref[...])

    a = _dot(a_ref[...], wa_ref[...])
    merged = (g_ref[:, 0:d_model] * a + g_ref[:, d_model:2 * d_model] * c
              + g_ref[:, 2 * d_model:3 * d_model] * p)
    o_ref[...] = x_ref[...] + _dot(merged.astype(BF16), wo_ref[...])


def _mix(x, u, z, attn, gates, conv_hist, pool_hist, lw, *, prompt, tl):
    n_seq, seq_len, d_model = x.shape
    n_tiles = seq_len // tl
    seq = lambda c: pl.BlockSpec((None, tl, c), lambda b, i: (b, i, 0))
    if prompt:
        uh_arr, zh_arr = u, z
        uh_spec = pl.BlockSpec((None, CONV_HALO, D_CONV),
                               lambda b, i: (b, jnp.maximum(i * (tl // CONV_HALO) - 1, 0), 0))
        zh_spec = pl.BlockSpec((None, POOL_HALO, D_POOL),
                               lambda b, i: (b, jnp.maximum(i * (tl // POOL_HALO) - 1, 0), 0))
        n_pool_prev = 0
    else:
        assert n_tiles == 1
        uh_arr = jnp.pad(conv_hist, ((0, 0), (CONV_HALO - conv_hist.shape[1], 0), (0, 0)))
        zh_arr = jnp.pad(pool_hist, ((0, 0), (POOL_HALO - pool_hist.shape[1], 0), (0, 0)))
        uh_spec = pl.BlockSpec((None, CONV_HALO, D_CONV), lambda b, i: (b, 0, 0))
        zh_spec = pl.BlockSpec((None, POOL_HALO, D_POOL), lambda b, i: (b, 0, 0))
        n_pool_prev = POOL_STATE
    const = lambda shape: _resident(shape, lambda b, i: (0,) * len(shape))
    return pl.pallas_call(
        functools.partial(_mix_kernel, prompt=prompt, n_pool_prev=n_pool_prev),
        grid=(n_seq, n_tiles),
        in_specs=[seq(d_model), seq(D_CONV), uh_spec, seq(D_POOL), zh_spec, seq(D_ATTN), seq(N_BRANCHES * d_model),
                  const((CONV_WIDTH, D_CONV)), const((1, D_CONV)), const((1, D_CONV)), const((1, D_CONV)),
                  const((D_CONV, d_model)), const((len(POOL_WINDOWS), POOL_GROUP, POOL_GROUP)),
                  const((1, D_POOL)), const((D_POOL, d_model)), const((D_ATTN, d_model)),
                  const((d_model, d_model))],
        out_specs=seq(d_model),
        out_shape=jax.ShapeDtypeStruct(x.shape, F32),
        scratch_shapes=[pltpu.VMEM((CONV_HALO + tl + SUBLANES, D_CONV), F32),
                        pltpu.VMEM((POOL_HALO + tl, D_POOL), F32)],
        compiler_params=_cparams(("parallel", "arbitrary")),
    )(x, u, uh_arr, z, zh_arr, attn, gates,
      lw["conv_w"], lw["conv_b"], lw["conv_ln_g"], lw["conv_ln_b"], lw["w_conv_out"], lw["pool_lin"],
      lw["pool_scale"], lw["w_pool_out"], lw["w_attn_out"], lw["w_out"])


def _ffn_chunk(d_ff):
    for n in (2, 4, 1):
        if d_ff % (n * LANES) == 0 and d_ff // n <= 2048:
            return d_ff // n
    return d_ff


def _ffn_kernel(x_ref, gain_ref, w1_ref, w3_ref, w2_ref, fg_ref, o_ref, *, chunk, final_norm):
    x = x_ref[...]
    h = _rms(x, gain_ref[...]).astype(BF16)
    acc = x
    for c0 in range(0, w1_ref.shape[1], chunk):
        act = jax.nn.silu(_dot(h, w1_ref[:, c0:c0 + chunk])) * _dot(h, w3_ref[:, c0:c0 + chunk])
        acc = acc + _dot(act.astype(BF16), w2_ref[c0:c0 + chunk, :])
    o_ref[...] = _rms(acc, fg_ref[...]) if final_norm else acc


def _ffn(x2d, gain, w1, w3, w2, final_gain, *, tm):
    t, d_model = x2d.shape
    d_ff = w1.shape[1]
    final_norm = final_gain is not None
    fg = (final_gain if final_norm else gain).reshape(1, d_model)
    row = pl.BlockSpec((tm, d_model), lambda i: (i, 0))
    vec = _resident((1, d_model), lambda i: (0, 0))
    return pl.pallas_call(
        functools.partial(_ffn_kernel, chunk=_ffn_chunk(d_ff), final_norm=final_norm),
        grid=(t // tm,),
        in_specs=[row, vec, _resident((d_model, d_ff), lambda i: (0, 0)), _resident((d_model, d_ff), lambda i: (0, 0)),
                  _resident((d_ff, d_model), lambda i: (0, 0)), vec],
        out_specs=row,
        out_shape=jax.ShapeDtypeStruct((t, d_model), F32),
        compiler_params=_cparams(("parallel",)),
    )(x2d, gain.reshape(1, d_model), w1, w3, w2, fg)


MOE_CHUNK = 256
MOE_TAIL = LANES


def _dot_halves(a_ref, b):
    half = a_ref.shape[0] // 2
    return jnp.concatenate([_dot(a_ref[:half, :], b), _dot(a_ref[half:, :], b)], axis=0)


def _router_kernel(x_ref, gain_ref, r_ref, tri_ref, rank_ref, rankt_ref, combt_ref, cnt_ref):
    tm = x_ref.shape[0]
    lane = lax.broadcasted_iota(jnp.int32, (tm, LANES), 1)
    h32 = _rms(x_ref[...], gain_ref[...])
    logits = jnp.dot(h32, r_ref[...], preferred_element_type=F32, precision=lax.Precision.HIGHEST)
    logits = jnp.where(lane < N_EXPERTS, logits, -jnp.inf)
    m1 = jnp.max(logits, axis=1, keepdims=True)
    i1 = jnp.min(jnp.where(logits == m1, lane, LANES), axis=1, keepdims=True)
    rest = jnp.where(lane == i1, -jnp.inf, logits)
    m2 = jnp.max(rest, axis=1, keepdims=True)
    i2 = jnp.min(jnp.where(rest == m2, lane, LANES), axis=1, keepdims=True)
    e2 = jnp.exp(m2 - m1)
    den = 1.0 + e2
    comb = jnp.where(lane == i1, 1.0 / den, 0.0) + jnp.where(lane == i2, e2 / den, 0.0)
    routed = (lane == i1) | (lane == i2)
    ones = jnp.where(routed, 1.0, 0.0)
    before = _dot(tri_ref[...], ones.astype(BF16))
    rank = jnp.where(routed, before, -1.0)
    rank_ref[...] = rank
    rankt_ref[...] = rank.T[:N_EXPERTS]
    combt_ref[...] = comb.T[:N_EXPERTS]
    cnt_ref[...] = jnp.sum(ones, axis=0, keepdims=True).astype(jnp.int32)


def _moe_kernel(cnt_ref, x_ref, gain_ref, rank_ref, rankt_ref, combt_ref, w1t_ref, w3t_ref, w2t_ref, fg_ref,
                *refs, final_norm, n_carried):
    o_ref, ht_scr, xt_scr, acct_scr, yt_scr = refs[n_carried:]
    i, e, f = pl.program_id(0), pl.program_id(1), pl.program_id(2)
    n_e, n_f = pl.num_programs(1), pl.num_programs(2)
    tm = x_ref.shape[0]
    n_tail = (jnp.maximum(cnt_ref[i * LANES + e] - MOE_CHUNK, 0) + MOE_TAIL - 1) // MOE_TAIL

    @pl.when((e == 0) & (f == 0))
    def _():
        ht_scr[...] = _rms(x_ref[...], gain_ref[...]).T.astype(BF16)
        yt_scr[...] = jnp.zeros_like(yt_scr)

    def expert_pass(base, width):
        cols = pl.ds(base if isinstance(base, int) else pl.multiple_of(base, MOE_TAIL), width)

        @pl.when(f == 0)
        def _():
            lane = lax.broadcasted_iota(jnp.int32, (tm, LANES), 1)
            rank_col = jnp.sum(jnp.where(lane == e, rank_ref[...], 0.0), axis=1, keepdims=True)
            slot = lax.broadcasted_iota(jnp.int32, (tm, width), 1) + base
            take = jnp.where(rank_col == slot.astype(F32), 1.0, 0.0).astype(BF16)
            xt_scr[:, cols] = _dot_halves(ht_scr, take).astype(BF16)

        xt = xt_scr[:, cols]
        act = jax.nn.silu(_dot(w1t_ref[...], xt)) * _dot(w3t_ref[...], xt)
        part = _dot_halves(w2t_ref, act.astype(BF16))

        @pl.when(f == 0)
        def _():
            acct_scr[:, cols] = part

        @pl.when(f > 0)
        def _():
            acct_scr[:, cols] += part

    expert_pass(0, MOE_CHUNK)

    def tail_pass(c, carry):
        expert_pass(MOE_CHUNK + c * MOE_TAIL, MOE_TAIL)
        return carry

    lax.fori_loop(0, n_tail, tail_pass, 0)

    @pl.when(f == n_f - 1)
    def _():
        rank_row = rankt_ref[pl.ds(e, 1), :]
        comb_row = combt_ref[pl.ds(e, 1), :]

        def scatter_pass(base, width):
            cols = pl.ds(base if isinstance(base, int) else pl.multiple_of(base, MOE_TAIL), width)
            slot = lax.broadcasted_iota(jnp.int32, (width, tm), 0) + base
            put = jnp.where(rank_row == slot.astype(F32), 1.0, 0.0).astype(BF16)
            a = acct_scr[:, cols]
            hi = a.astype(BF16)
            lo = (a - hi.astype(F32)).astype(BF16)
            yt_scr[...] += comb_row * (_dot(hi, put) + _dot(lo, put))

        scatter_pass(0, MOE_CHUNK)

        def tail_scatter(c, carry):
            scatter_pass(MOE_CHUNK + c * MOE_TAIL, MOE_TAIL)
            return carry

        lax.fori_loop(0, n_tail, tail_scatter, 0)

    @pl.when((e == n_e - 1) & (f == n_f - 1))
    def _():
        out = x_ref[...] + yt_scr[...].T
        o_ref[...] = _rms(out, fg_ref[...]) if final_norm else out


def _moe_tiling(t):
    main, rest = 7 * LANES, MOE_CHUNK
    if t > rest and (t - rest) % main == 0:
        return [(main, 0, (t - rest) // main), (rest, (t - rest) // rest, 1)]
    tm = _pick_tile(t, (1024, 512, MOE_CHUNK))
    return [(tm, 0, t // tm)]


def _moe(x2d, gain, router, w1t, w3t, w2t, final_gain, *, tm, tile0, n_tiles, out_carried=None):
    t, d_model = x2d.shape
    n_e, d_ff, _ = w1t.shape
    assert n_e == N_EXPERTS and tm % LANES == 0 and tm >= MOE_CHUNK
    chunk = _ffn_chunk(d_ff)
    n_f = d_ff // chunk
    final_norm = final_gain is not None
    fg = (final_gain if final_norm else gain).reshape(1, d_model)
    gain2 = gain.reshape(1, d_model)
    r_pad = jnp.pad(router, ((0, 0), (0, LANES - n_e)))
    earlier = (jnp.arange(tm)[None, :] < jnp.arange(tm)[:, None]).astype(BF16)

    rank, rank_t, comb_t, counts = pl.pallas_call(
        _router_kernel,
        grid=(n_tiles,),
        in_specs=[pl.BlockSpec((tm, d_model), lambda i: (i + tile0, 0)), _resident((1, d_model), lambda i: (0, 0)),
                  _resident((d_model, LANES), lambda i: (0, 0)), _resident((tm, tm), lambda i: (0, 0))],
        out_specs=[pl.BlockSpec((tm, LANES), lambda i: (i, 0)),
                   pl.BlockSpec((None, N_EXPERTS, tm), lambda i: (i, 0, 0)),
                   pl.BlockSpec((None, N_EXPERTS, tm), lambda i: (i, 0, 0)),
                   pl.BlockSpec((None, 1, LANES), lambda i: (i, 0, 0))],
        out_shape=[jax.ShapeDtypeStruct((n_tiles * tm, LANES), F32),
                   jax.ShapeDtypeStruct((n_tiles, N_EXPERTS, tm), F32),
                   jax.ShapeDtypeStruct((n_tiles, N_EXPERTS, tm), F32),
                   jax.ShapeDtypeStruct((n_tiles, 1, LANES), jnp.int32)],
        compiler_params=_cparams(("parallel",)),
    )(x2d, gain2, r_pad, earlier)

    row = pl.BlockSpec((tm, d_model), lambda i, e, f, cnt: (i + tile0, 0))
    vec = _resident((1, d_model), lambda i, e, f, cnt: (0, 0))
    tile_t = pl.BlockSpec((None, N_EXPERTS, tm), lambda i, e, f, cnt: (i, 0, 0))
    carried = () if out_carried is None else (out_carried,)
    grid_spec = pltpu.PrefetchScalarGridSpec(
        num_scalar_prefetch=1,
        grid=(n_tiles, n_e, n_f),
        in_specs=[row, vec, pl.BlockSpec((tm, LANES), lambda i, e, f, cnt: (i, 0)), tile_t, tile_t,
                  pl.BlockSpec((None, chunk, d_model), lambda i, e, f, cnt: (e, f, 0)),
                  pl.BlockSpec((None, chunk, d_model), lambda i, e, f, cnt: (e, f, 0)),
                  pl.BlockSpec((None, d_model, chunk), lambda i, e, f, cnt: (e, 0, f)), vec]
                 + [pl.BlockSpec(memory_space=pl.ANY)] * len(carried),
        out_specs=row,
        scratch_shapes=[pltpu.VMEM((d_model, tm), BF16), pltpu.VMEM((d_model, tm), BF16),
                        pltpu.VMEM((d_model, tm), F32), pltpu.VMEM((d_model, tm), F32)],
    )
    operands = (counts.reshape(n_tiles * LANES), x2d, gain2, rank, rank_t, comb_t, w1t, w3t, w2t, fg) + carried
    return pl.pallas_call(
        functools.partial(_moe_kernel, final_norm=final_norm, n_carried=len(carried)),
        grid_spec=grid_spec,
        out_shape=jax.ShapeDtypeStruct((t, d_model), F32),
        input_output_aliases={len(operands) - 1: 0} if carried else {},
        compiler_params=_cparams(("parallel", "arbitrary", "arbitrary")),
    )(*operands)


def _pick_tile(n, prefs):
    for p in prefs:
        if n % p == 0:
            return p
    return n


def _forward(x, pos, cache, layers, norm_final, *, prompt):
    n_seq, seq_len, d_model = x.shape
    t = n_seq * seq_len
    depth = len(layers)
    if prompt:
        assert seq_len % MOBA_BLOCK == 0 and seq_len // MOBA_BLOCK <= HEAD_DIM
        tm_in = _pick_tile(seq_len, (512, MOBA_BLOCK))
        tables = _rope_tables(pos)
        tl = _pick_tile(seq_len, (512, 256, 128, 64, 32))
        tm_ffn = _pick_tile(t, (512, 256))
        tm_moe = _pick_tile(t, (1024, 512, 256))
    else:
        tm_in = t
        tables = tuple(jnp.tile(tb, (n_seq, 1)) for tb in _rope_tables(pos))
        tl = seq_len
        tm_ffn = tm_moe = t
        cache_kt, cache_vt, page_table, state_conv, state_pool = cache
    ks, vs, cs, ps = [], [], [], []
    kv_t = None
    for l, lw in enumerate(layers):
        outs = _inproj(x.reshape(t, d_model), lw["norm_mix"], lw["w_in_aug"], tables,
                       prompt=prompt, seq_len=seq_len, tm=tm_in, layer=l, depth=depth, kv_carried=kv_t)
        if prompt:
            q_aug, k_aug, kt_all, vt_all, v_bf, kmean, u, z, gates = outs
            kv_t = (kt_all, vt_all)
            attn = _attn_prompt(q_aug, k_aug, v_bf, kmean, n_seq=n_seq, seq_len=seq_len, tq=2 * MOBA_BLOCK)
            conv_hist = pool_hist = None
        else:
            q_aug, k_new, v_new, u, z, gates = outs
            attn = _attn_sample(q_aug, k_new, v_new, cache_kt, cache_vt, page_table, l,
                                n_seq=n_seq, seq_len=seq_len)
            conv_hist, pool_hist = state_conv[l], state_pool[l]
        r3 = lambda a: a.reshape(n_seq, seq_len, a.shape[-1])
        u3, z3 = r3(u), r3(z)
        x = _mix(x, u3, z3, r3(attn), r3(gates), conv_hist, pool_hist, lw, prompt=prompt, tl=tl)
        final_gain = norm_final if l == depth - 1 else None
        x2d = x.reshape(t, d_model)
        if lw["kind"] == "dense":
            x2d = _ffn(x2d, lw["norm_ffn"], lw["w1"], lw["w3"], lw["w2"], final_gain, tm=tm_ffn)
        else:
            out = None
            for tm_moe, tile0, n_tiles in _moe_tiling(t):
                out = _moe(x2d, lw["norm_ffn"], lw["router"], lw["w1"], lw["w3"], lw["w2"], final_gain,
                           tm=tm_moe, tile0=tile0, n_tiles=n_tiles, out_carried=out)
            x2d = out
        x = x2d.reshape(n_seq, seq_len, d_model)
        if prompt:
            cs.append(u3[:, seq_len - (CONV_WIDTH - 1):])
            ps.append(z3[:, seq_len - POOL_STATE:])
        else:
            ks.append(k_new.reshape(n_seq, seq_len, N_HEADS, HEAD_DIM))
            vs.append(v_new.reshape(n_seq, seq_len, N_HEADS, HEAD_DIM))
            cs.append(jnp.concatenate([conv_hist, u3], axis=1)[:, -(CONV_WIDTH - 1):])
            ps.append(jnp.concatenate([pool_hist, z3], axis=1)[:, -POOL_STATE:])
    if prompt:
        untranspose = lambda a: a.reshape(depth, n_seq, N_HEADS, HEAD_DIM, seq_len).transpose(0, 1, 4, 2, 3)
        k_out, v_out = untranspose(kv_t[0]), untranspose(kv_t[1])
    else:
        k_out, v_out = jnp.stack(ks), jnp.stack(vs)
    return x, k_out, v_out, jnp.stack(cs), jnp.stack(ps)


def kernel(x_prompt, x_sample, cache_k, cache_v, page_table, state_conv, state_pool, norm_mix, norm_ffn, norm_final, w_in, w_attn_out, conv_w, conv_b, conv_ln_g, conv_ln_b, w_conv_out, pool_lin, pool_scale, w_pool_out, w_out, ffn_w1, ffn_w3, ffn_w2, moe_router, moe_w1, moe_w3, moe_w2):
    depth = norm_mix.shape[0]
    layers = []
    for l in range(depth):
        lw = dict(
            norm_mix=norm_mix[l], norm_ffn=norm_ffn[l],
            w_in_aug=_prep_w_in(w_in[l]),
            w_attn_out=w_attn_out[l].astype(BF16), conv_w=conv_w[l], conv_b=conv_b[l].reshape(1, -1),
            conv_ln_g=conv_ln_g[l].reshape(1, -1), conv_ln_b=conv_ln_b[l].reshape(1, -1),
            w_conv_out=w_conv_out[l].astype(BF16), pool_lin=pool_lin[l].astype(BF16),
            pool_scale=pool_scale[l].reshape(1, -1), w_pool_out=w_pool_out[l].astype(BF16),
            w_out=w_out[l].astype(BF16))
        i = l // 2
        if l % 2 == 0:
            lw.update(kind="dense", w1=ffn_w1[i].astype(BF16), w3=ffn_w3[i].astype(BF16), w2=ffn_w2[i].astype(BF16))
        else:
            tr = lambda w: jnp.swapaxes(w, 1, 2).astype(BF16)
            lw.update(kind="moe", router=moe_router[i], w1=tr(moe_w1[i]), w3=tr(moe_w3[i]), w2=tr(moe_w2[i]))
        layers.append(lw)

    past_len = page_table.shape[1] * cache_k.shape[2]
    pos_prompt = jnp.arange(x_prompt.shape[1], dtype=jnp.int32)
    pos_sample = past_len + jnp.arange(x_sample.shape[1], dtype=jnp.int32)
    y_p, k_p, v_p, conv_p, pool_p = _forward(x_prompt, pos_prompt, None, layers, norm_final, prompt=True)
    cache = (_transposed_pages(cache_k), _transposed_pages(cache_v), page_table, state_conv, state_pool)
    y_s, k_s, v_s, conv_s, pool_s = _forward(x_sample, pos_sample, cache, layers, norm_final, prompt=False)
    return (y_p, y_s, k_p, v_p, k_s, v_s, conv_p, conv_s, pool_p, pool_s)
```

```python
import functools
import math

import jax
import jax.numpy as jnp
import numpy as np
from jax import lax
from jax.experimental import pallas as pl
from jax.experimental.pallas import tpu as pltpu

F32 = jnp.float32
BF16 = jnp.bfloat16

N_HEADS = 8
HEAD_DIM = 64
D_ATTN = N_HEADS * HEAD_DIM
ROPE_DIM = HEAD_DIM // 4
ROPE_THETA = 500000.0
MOBA_BLOCK = 256
MOBA_TOPK = 3
D_CONV = 512
CONV_WIDTH = 31
D_POOL = 512
POOL_WINDOWS = (2, 4, 8, 16)
POOL_GROUP = D_POOL // len(POOL_WINDOWS)
POOL_STATE = max(POOL_WINDOWS) - 1
N_BRANCHES = 3
N_EXPERTS = 8
EPS = 1e-6

LANES = 128
SUBLANES = 8
SLAB = 2 * HEAD_DIM
CONV_HALO = 32
POOL_HALO = 16
NEG = -1e30
VMEM_LIMIT = 56 * 1024 * 1024


def _cparams(sem):
    return pltpu.CompilerParams(dimension_semantics=sem, vmem_limit_bytes=VMEM_LIMIT)


def _resident(shape, index_map):
    return pl.BlockSpec(shape, index_map, pipeline_mode=pl.Buffered(1))


def _rms(x, gain):
    ms = jnp.mean(x * x, axis=-1, keepdims=True)
    return x * lax.rsqrt(ms + EPS) * gain


def _dot(a, b):
    return jnp.dot(a, b, preferred_element_type=F32)


def _dot_t(a, b):
    return lax.dot_general(a, b, (((1,), (1,)), ((), ())), preferred_element_type=F32)


_C_Q = 0
_C_K = _C_Q + D_ATTN
_C_V = _C_K + D_ATTN
_C_GLU = _C_V + D_ATTN
_C_PZ = _C_GLU + 2 * D_CONV
_C_G = _C_PZ + D_POOL


def _prep_w_in(w_in):
    is_q = (jnp.arange(w_in.shape[1]) < D_ATTN)[None, :]
    log2e = math.log2(math.e)
    hi = float(np.float32(log2e))
    lo = float(np.float32(log2e - hi))
    wq = w_in * (HEAD_DIM ** -0.5)
    return jnp.where(is_q, wq * hi + wq * lo, w_in).astype(BF16)


def _rope_tables(pos):
    half = ROPE_DIM // 2
    step = -2.0 * math.log(ROPE_THETA) / ROPE_DIM
    step_hi = float(np.float32(step))
    step_lo = float(np.float32(step - step_hi))
    idx = jnp.arange(half, dtype=F32)
    inv_freq = jnp.exp(idx * step_hi + idx * step_lo)
    ang = pos.astype(F32)[:, None] * inv_freq[None, :]
    cos, sin = jnp.cos(ang), jnp.sin(ang)
    n = pos.shape[0]
    ones = jnp.ones((n, HEAD_DIM - ROPE_DIM), F32)
    zeros8 = jnp.zeros((n, half), F32)
    zeros48 = jnp.zeros((n, HEAD_DIM - ROPE_DIM), F32)
    c = jnp.concatenate([cos, cos, ones], axis=1)
    s_up = jnp.concatenate([zeros8, sin, zeros48], axis=1)
    s_dn = jnp.concatenate([-sin, zeros8, zeros48], axis=1)
    tile2 = lambda t: jnp.concatenate([t, t], axis=1)
    return tile2(c), tile2(s_up), tile2(s_dn)


def _inproj_kernel(x_ref, gain_ref, w_ref, cos_ref, sup_ref, sdn_ref, *refs, prompt, tiles_per_seq, n_carried):
    out_refs = refs[n_carried:]
    if prompt:
        qa_ref, ka_ref, k_ref, v_ref, vb_ref, km_ref, u_ref, pz_ref, g_ref = out_refs
    else:
        qa_ref, k_ref, v_ref, u_ref, pz_ref, g_ref = out_refs
    tm = x_ref.shape[0]
    h = _rms(x_ref[...], gain_ref[...]).astype(BF16)
    cosv, sup, sdn = cos_ref[...], sup_ref[...], sdn_ref[...]

    def rope(slab):
        return slab * cosv + pltpu.roll(slab, ROPE_DIM // 2, 1) * sup + pltpu.roll(slab, LANES - ROPE_DIM // 2, 1) * sdn

    lane = lax.broadcasted_iota(jnp.int32, (tm, LANES), 1)
    first = lane < HEAD_DIM
    q_all = _dot(h, w_ref[:, _C_Q:_C_K])
    for pair in range(N_HEADS // 2):
        q_nat = rope(q_all[:, pair * LANES:(pair + 1) * LANES])
        qa_ref[:, (2 * pair) * SLAB:(2 * pair + 1) * SLAB] = jnp.where(first, q_nat, 0.0).astype(BF16)
        qa_ref[:, (2 * pair + 1) * SLAB:(2 * pair + 2) * SLAB] = jnp.where(first, 0.0, q_nat).astype(BF16)

    k_all = _dot(h, w_ref[:, _C_K:_C_V])
    if prompt:
        row = lax.broadcasted_iota(jnp.int32, (tm, LANES), 0)
        blk = ((pl.program_id(0) % tiles_per_seq) * tm + row) // MOBA_BLOCK
        onehot_even = (lane == blk + HEAD_DIM).astype(F32)
        onehot_odd = (lane == blk).astype(F32)
    for pair in range(N_HEADS // 2):
        k_nat = rope(k_all[:, pair * LANES:(pair + 1) * LANES])
        if not prompt:
            k_ref[:, pair * LANES:(pair + 1) * LANES] = k_nat
        else:
            k_ref[pair * LANES:(pair + 1) * LANES, :] = k_nat.T
            k_even, k_odd = jnp.where(first, k_nat, 0.0), jnp.where(first, 0.0, k_nat)
            for hd, k_slab, onehot in ((2 * pair, k_even, onehot_even), (2 * pair + 1, k_odd, onehot_odd)):
                ka_ref[:, hd * SLAB:(hd + 1) * SLAB] = (k_slab + onehot).astype(BF16)
                km = jnp.mean(k_slab.reshape(tm // MOBA_BLOCK, MOBA_BLOCK, SLAB), axis=1)
                km_ref[:, 0, hd * SLAB:(hd + 1) * SLAB] = km

    v_all = _dot(h, w_ref[:, _C_V:_C_GLU])
    if prompt:
        vb_ref[...] = v_all.astype(BF16)
        for c in range(D_ATTN // LANES):
            v_ref[c * LANES:(c + 1) * LANES, :] = v_all[:, c * LANES:(c + 1) * LANES].T
    else:
        v_ref[...] = v_all

    glu = _dot(h, w_ref[:, _C_GLU:_C_PZ])
    u_ref[...] = glu[:, :D_CONV] * jax.nn.sigmoid(glu[:, D_CONV:])
    pz_ref[...] = _dot(h, w_ref[:, _C_PZ:_C_G])
    g_ref[...] = jax.nn.sigmoid(_dot(h, w_ref[:, _C_G:]))


def _inproj(x2d, gain, w_aug, tables, *, prompt, seq_len, tm, layer=0, depth=1, kv_carried=None):
    t, d_model = x2d.shape
    n_tiles = t // tm
    cos_t, sup_t, sdn_t = tables
    n_tab = cos_t.shape[0] // tm
    tiles_per_seq = max(seq_len // tm, 1)
    n_cols = w_aug.shape[1]
    row = lambda c: pl.BlockSpec((tm, c), lambda i: (i, 0))
    tab = pl.BlockSpec((tm, LANES), lambda i: (i % n_tab, 0))
    f = jax.ShapeDtypeStruct
    outs = [(f((t, N_HEADS * SLAB), BF16), row(N_HEADS * SLAB))]
    if prompt:
        outs.append((f((t, N_HEADS * SLAB), BF16), row(N_HEADS * SLAB)))
        kv_t = (f((depth, t // seq_len, D_ATTN, seq_len), F32),
                pl.BlockSpec((None, None, D_ATTN, tm), lambda i: (layer, i // tiles_per_seq, 0, i % tiles_per_seq)))
        outs += [kv_t, kv_t]
    else:
        outs += [(f((t, D_ATTN), F32), row(D_ATTN)), (f((t, D_ATTN), F32), row(D_ATTN))]
    if prompt:
        bpt = tm // MOBA_BLOCK
        outs.append((f((t, D_ATTN), BF16), row(D_ATTN)))
        outs.append((f((t // MOBA_BLOCK, 1, N_HEADS * SLAB), F32),
                     pl.BlockSpec((bpt, 1, N_HEADS * SLAB), lambda i: (i, 0, 0))))
    outs += [(f((t, D_CONV), F32), row(D_CONV)), (f((t, D_POOL), F32), row(D_POOL)),
             (f((t, N_BRANCHES * d_model), F32), row(N_BRANCHES * d_model))]
    carried = tuple(kv_carried) if kv_carried is not None else ()
    n_in = 6
    return pl.pallas_call(
        functools.partial(_inproj_kernel, prompt=prompt, tiles_per_seq=tiles_per_seq, n_carried=len(carried)),
        grid=(n_tiles,),
        in_specs=[row(d_model), _resident((1, d_model), lambda i: (0, 0)),
                  _resident((d_model, n_cols), lambda i: (0, 0)), tab, tab, tab]
                 + [pl.BlockSpec(memory_space=pl.ANY)] * len(carried),
        out_specs=[o[1] for o in outs],
        out_shape=[o[0] for o in outs],
        input_output_aliases={n_in + j: 2 + j for j in range(len(carried))},
        compiler_params=_cparams(("parallel",)),
    )(x2d, gain.reshape(1, d_model), w_aug, cos_t, sup_t, sdn_t, *carried)


def _top_blocks(gate, valid, idx, axis):
    g = jnp.where(valid, gate, -jnp.inf)
    picked = jnp.zeros(gate.shape, jnp.bool_)
    for _ in range(MOBA_TOPK):
        m = jnp.max(g, axis=axis, keepdims=True)
        first = jnp.min(jnp.where(g == m, idx, jnp.int32(1 << 20)), axis=axis, keepdims=True)
        pick = (idx == first) & (m > -jnp.inf)
        picked = picked | pick
        g = jnp.where(pick, -jnp.inf, g)
    return picked


def _attn_prompt_kernel(q_ref, k_ref, v_ref, km_ref, o_ref):
    ti = pl.program_id(1)
    tq = q_ref.shape[0]
    nb = km_ref.shape[0]
    span = 2 * MOBA_BLOCK
    lane = lax.broadcasted_iota(jnp.int32, (tq, LANES), 1)
    blk_row = lax.broadcasted_iota(jnp.int32, (nb, tq), 0)
    own_t = ti * (tq // MOBA_BLOCK) + lax.broadcasted_iota(jnp.int32, (nb, tq), 1) // MOBA_BLOCK
    row = lax.broadcasted_iota(jnp.int32, (tq, span), 0)
    col = lax.broadcasted_iota(jnp.int32, (tq, span), 1)

    q_full = []
    for hd in range(N_HEADS):
        qs = q_ref[:, hd * SLAB:(hd + 1) * SLAB]
        gate_t = _dot_t(km_ref[:, hd * SLAB:(hd + 1) * SLAB].astype(BF16), qs)
        picked = _top_blocks(gate_t, blk_row < own_t, blk_row, 0)
        pen_t = jnp.where(picked | (blk_row == own_t), 0.0, NEG)
        pieces = [pen_t]
        if nb < HEAD_DIM:
            pieces.append(jnp.full((HEAD_DIM - nb, tq), NEG, F32))
        zeros = jnp.zeros((HEAD_DIM, tq), F32)
        pieces = [zeros] + pieces if hd % 2 == 0 else pieces + [zeros]
        pen = jnp.concatenate(pieces, axis=0).T
        q_full.append((qs.astype(F32) + pen).astype(BF16))

    def sweep(start, carry, diag, width=span):
        start = start if isinstance(start, int) else pl.multiple_of(start, span)
        new = []
        for pair in range(N_HEADS // 2):
            vb = v_ref[pl.ds(start, width), pair * LANES:(pair + 1) * LANES]
            upd = []
            for hd in (2 * pair, 2 * pair + 1):
                s = _dot_t(q_full[hd], k_ref[pl.ds(start, width), hd * SLAB:(hd + 1) * SLAB])
                if diag is not None:
                    s = jnp.where(col + diag * span <= row, s, NEG)
                s_max = jnp.max(s, axis=1, keepdims=True)
                if carry is None:
                    m_new, alpha = s_max, None
                else:
                    m_old = carry[3 * pair + (hd % 2)]
                    m_new = jnp.maximum(m_old, s_max)
                    alpha = jnp.exp2(m_old - m_new)
                upd.append((m_new, alpha, jnp.exp2(s - m_new).astype(BF16)))
            v_ext = jnp.concatenate([vb, jnp.ones((width, LANES), BF16)], axis=1)
            pv2 = _dot(jnp.concatenate([upd[0][2], upd[1][2]], axis=0), v_ext)
            pv = jnp.where(lane < HEAD_DIM, pv2[:tq, :LANES], pv2[tq:, :LANES])
            l0, l1 = pv2[:tq, LANES:], pv2[tq:, LANES:]
            if carry is None:
                acc = pv
            else:
                l0 = upd[0][1] * carry[3 * pair + 2][0] + l0
                l1 = upd[1][1] * carry[3 * pair + 2][1] + l1
                acc = jnp.where(lane < HEAD_DIM, upd[0][1], upd[1][1]) * carry[3 * pair + 2][2] + pv
            new += [upd[0][0], upd[1][0], (l0, l1, acc)]
        return tuple(new)

    state = None
    for d in range(tq // span):
        state = sweep(ti * tq + d * span, state, d)
    n_past = ti * (tq // span)
    odd = n_past % 2
    state = lax.fori_loop(0, odd, lambda i, c: sweep(0, c, None), state)
    state = lax.fori_loop(0, n_past // 2, lambda i, c: sweep((odd + 2 * i) * span, c, None, 2 * span), state)
    for pair in range(N_HEADS // 2):
        l0, l1, acc = state[3 * pair + 2]
        o_ref[:, pair * LANES:(pair + 1) * LANES] = (acc / jnp.where(lane < HEAD_DIM, l0, l1)).astype(o_ref.dtype)


def _attn_prompt(q_aug, k_aug, v_bf, kmean, *, n_seq, seq_len, tq):
    t = q_aug.shape[0]
    nb = seq_len // MOBA_BLOCK
    n_tiles = seq_len // tq
    assert tq % (2 * MOBA_BLOCK) == 0 and seq_len % tq == 0
    km = kmean.reshape(n_seq, nb, N_HEADS * SLAB)
    return pl.pallas_call(
        _attn_prompt_kernel,
        grid=(n_seq, n_tiles),
        in_specs=[
            pl.BlockSpec((tq, N_HEADS * SLAB), lambda b, i: (b * n_tiles + i, 0)),
            _resident((seq_len, N_HEADS * SLAB), lambda b, i: (b, 0)),
            _resident((seq_len, D_ATTN), lambda b, i: (b, 0)),
            pl.BlockSpec((None, nb, N_HEADS * SLAB), lambda b, i: (b, 0, 0)),
        ],
        out_specs=pl.BlockSpec((tq, D_ATTN), lambda b, i: (b * n_tiles + i, 0)),
        out_shape=jax.ShapeDtypeStruct((t, D_ATTN), BF16),
        compiler_params=_cparams(("parallel", "arbitrary")),
    )(q_aug, k_aug, v_bf, km)


def _attn_sample_kernel(pt_ref, q_ref, kn_ref, vn_ref, e_ref, kc_ref, vc_ref, o_ref, buf, sem, s_scr, p_scr,
                        *, layer, pages_per_step, n_steps, page_size):
    b = pl.program_id(0)
    n_seq = pl.num_programs(0)
    ls = q_ref.shape[0]
    rows = N_HEADS * ls
    step_keys = pages_per_step * page_size
    pages_per_block = MOBA_BLOCK // page_size
    blocks_per_step = pages_per_step // pages_per_block
    n_fetch = 2 * n_steps
    lane_d = lax.broadcasted_iota(jnp.int32, (ls, D_ATTN), 1)
    lane = lax.broadcasted_iota(jnp.int32, (rows, LANES), 1)

    def page_copy(seq, f, i):
        cache = kc_ref if f < n_steps else vc_ref
        page = pt_ref[seq, (f % n_steps) * pages_per_step + i]
        return pltpu.make_async_copy(cache.at[layer, page], buf.at[f % 2, i], sem.at[f % 2])

    def start_fetch(seq, f):
        for i in range(pages_per_step):
            page_copy(seq, f, i).start(priority=i % 2)

    def wait_fetch(seq, f):
        for i in range(pages_per_step):
            page_copy(seq, f, i).wait()

    def step_pages(f):
        return jnp.concatenate([buf[f % 2, i].astype(BF16) for i in range(pages_per_step)], axis=1)

    @pl.when(b == 0)
    def _():
        start_fetch(b, 0)

    qa = q_ref[...]
    lane_q = lax.broadcasted_iota(jnp.int32, (ls, LANES), 1)
    q_nat = jnp.concatenate(
        [jnp.where(lane_q < HEAD_DIM, qa[:, (2 * p) * SLAB:(2 * p + 1) * SLAB],
                   qa[:, (2 * p + 1) * SLAB:(2 * p + 2) * SLAB]) for p in range(N_HEADS // 2)], axis=1)
    qm = jnp.concatenate(
        [jnp.where(lane_d // HEAD_DIM == hd, q_nat, jnp.zeros_like(q_nat)) for hd in range(N_HEADS)], axis=0)

    gate = jnp.zeros((rows, LANES), F32)
    for f in range(n_steps):
        start_fetch(b, f + 1)
        wait_fetch(b, f)
        s = _dot(qm, step_pages(f))
        s_scr[:, f * step_keys:(f + 1) * step_keys] = s
        for blk in range(blocks_per_step):
            tot = jnp.sum(s[:, blk * MOBA_BLOCK:(blk + 1) * MOBA_BLOCK], axis=1, keepdims=True)
            gate = gate + jnp.where(lane == f * blocks_per_step + blk, tot, 0.0)

    picked = _top_blocks(gate * (1.0 / MOBA_BLOCK), lane < n_steps * blocks_per_step, lane, 1)
    sel = _dot(jnp.where(picked, 1.0, 0.0).astype(BF16), e_ref[...])
    s = jnp.where(sel > 0.5, s_scr[...], NEG)
    pad = jnp.zeros((LANES - ls, D_ATTN), BF16)
    s_own = _dot_t(qm, jnp.concatenate([kn_ref[...].astype(BF16), pad], axis=0))
    qi = lax.broadcasted_iota(jnp.int32, (rows, LANES), 0) % ls
    s_own = jnp.where(lane <= qi, s_own, NEG)
    m = jnp.maximum(jnp.max(s, axis=1, keepdims=True), jnp.max(s_own, axis=1, keepdims=True))
    p = jnp.exp2(s - m)
    p_own = jnp.exp2(s_own - m)
    l = jnp.sum(p, axis=1, keepdims=True) + jnp.sum(p_own, axis=1, keepdims=True)
    p_scr[...] = p.astype(BF16)
    acc = _dot(p_own.astype(BF16), jnp.concatenate([vn_ref[...].astype(BF16), pad], axis=0))

    for f in range(n_steps, n_fetch):
        if f + 1 < n_fetch:
            start_fetch(b, f + 1)
        else:
            @pl.when(b + 1 < n_seq)
            def _():
                start_fetch(b + 1, 0)
        wait_fetch(b, f)
        g = f - n_steps
        acc = acc + _dot_t(p_scr[:, g * step_keys:(g + 1) * step_keys], step_pages(f))

    o = acc / l
    out = jnp.zeros((ls, D_ATTN), F32)
    for hd in range(N_HEADS):
        out = out + jnp.where(lane_d // HEAD_DIM == hd, o[hd * ls:(hd + 1) * ls, :], 0.0)
    o_ref[...] = out.astype(o_ref.dtype)


def _attn_sample(q_aug, k_new, v_new, cache_kt, cache_vt, page_table, layer, *, n_seq, seq_len):
    n_pages = page_table.shape[1]
    page_size = cache_kt.shape[3]
    past_len = n_pages * page_size
    assert past_len % MOBA_BLOCK == 0 and MOBA_BLOCK % page_size == 0 and seq_len % 8 == 0
    assert seq_len <= MOBA_BLOCK and past_len // MOBA_BLOCK <= LANES
    pages_per_step = _pick_tile(n_pages, (16, 8, MOBA_BLOCK // page_size))
    n_steps = n_pages // pages_per_step
    rows = N_HEADS * seq_len
    expand = (jnp.arange(past_len)[None, :] // MOBA_BLOCK == jnp.arange(LANES)[:, None]).astype(BF16)

    grid_spec = pltpu.PrefetchScalarGridSpec(
        num_scalar_prefetch=1,
        grid=(n_seq,),
        in_specs=[pl.BlockSpec((seq_len, N_HEADS * SLAB), lambda b, pt: (b, 0)),
                  pl.BlockSpec((seq_len, D_ATTN), lambda b, pt: (b, 0)),
                  pl.BlockSpec((seq_len, D_ATTN), lambda b, pt: (b, 0)),
                  _resident((LANES, past_len), lambda b, pt: (0, 0)),
                  pl.BlockSpec(memory_space=pl.ANY), pl.BlockSpec(memory_space=pl.ANY)],
        out_specs=pl.BlockSpec((seq_len, D_ATTN), lambda b, pt: (b, 0)),
        scratch_shapes=[pltpu.VMEM((2, pages_per_step, D_ATTN, page_size), F32), pltpu.SemaphoreType.DMA((2,)),
                        pltpu.VMEM((rows, past_len), F32), pltpu.VMEM((rows, past_len), BF16)],
    )
    return pl.pallas_call(
        functools.partial(_attn_sample_kernel, layer=layer, pages_per_step=pages_per_step, n_steps=n_steps,
                          page_size=page_size),
        grid_spec=grid_spec,
        out_shape=jax.ShapeDtypeStruct((n_seq * seq_len, D_ATTN), BF16),
        compiler_params=_cparams(("arbitrary",)),
    )(page_table, q_aug, k_new, v_new, expand, cache_kt, cache_vt)


def _transposed_pages(cache):
    depth, n_pool, page_size = cache.shape[:3]
    return jnp.transpose(cache, (0, 1, 3, 4, 2)).reshape(depth, n_pool, D_ATTN, page_size)


def _mix_kernel(x_ref, u_ref, uh_ref, z_ref, zh_ref, a_ref, g_ref,
                cw_ref, cb_ref, lg_ref, lb_ref, wc_ref, pl_ref, ps_ref, wp_ref, wa_ref, wo_ref,
                o_ref, ext_ref, zext_ref, *, prompt, n_pool_prev):
    tl = u_ref.shape[0]
    i = pl.program_id(1)
    d_model = x_ref.shape[1]

    uh, zh = uh_ref[...], zh_ref[...]
    if prompt:
        uh = jnp.where(i == 0, 0.0, uh)
        zh = jnp.where(i == 0, 0.0, zh)
    ext_ref[0:CONV_HALO, :] = uh
    ext_ref[CONV_HALO:CONV_HALO + tl, :] = u_ref[...]
    zext_ref[0:POOL_HALO, :] = zh
    z = z_ref[...]
    zext_ref[POOL_HALO:, :] = z

    ext_ref[CONV_HALO + tl:, :] = jnp.zeros((SUBLANES, D_CONV), F32)
    base = CONV_HALO - (CONV_WIDTH - 1)
    groups = [None] * SUBLANES
    for j in range(CONV_WIDTH):
        k, r = divmod(base + j, SUBLANES)
        term = ext_ref[SUBLANES * k:SUBLANES * k + tl + SUBLANES, :] * cw_ref[j:j + 1, :]
        groups[r] = term if groups[r] is None else groups[r] + term
    y = jnp.zeros((tl, D_CONV), F32) + cb_ref[...]
    for r, grp in enumerate(groups):
        if grp is not None:
            y = y + grp[r:r + tl, :]
    mu = jnp.mean(y, axis=-1, keepdims=True)
    yc = y - mu
    var = jnp.mean(yc * yc, axis=-1, keepdims=True)
    yn = yc * lax.rsqrt(var + EPS) * lg_ref[...] + lb_ref[...]
    c = _dot(jax.nn.silu(yn).astype(BF16), wc_ref[...])

    pos = i * tl + lax.broadcasted_iota(jnp.int32, (tl, 1), 0)
    parts = []
    for gi, w in enumerate(POOL_WINDOWS):
        lo, hi = gi * POOL_GROUP, (gi + 1) * POOL_GROUP
        tot = z[:, lo:hi]
        for back in range(1, w):
            tot = tot + zext_ref[POOL_HALO - back:POOL_HALO - back + tl, lo:hi]
        cnt = jnp.minimum(w, pos + 1 + n_pool_prev).astype(F32)
        d = tot / cnt - z[:, lo:hi]
        parts.append(_dot(d.astype(BF16), pl_ref[gi]))
    pm = jnp.concatenate(parts, axis=1) * ps_ref[...]
    p = _dot(pm.astype(BF16), wp_ref[...])

    a = _dot(a_ref[...], wa_ref[...])
    merged = (g_ref[:, 0:d_model] * a + g_ref[:, d_model:2 * d_model] * c
              + g_ref[:, 2 * d_model:3 * d_model] * p)
    o_ref[...] = x_ref[...] + _dot(merged.astype(BF16), wo_ref[...])


def _mix(x, u, z, attn, gates, conv_hist, pool_hist, lw, *, prompt, tl):
    n_seq, seq_len, d_model = x.shape
    n_tiles = seq_len // tl
    seq = lambda c: pl.BlockSpec((None, tl, c), lambda b, i: (b, i, 0))
    if prompt:
        uh_arr, zh_arr = u, z
        uh_spec = pl.BlockSpec((None, CONV_HALO, D_CONV),
                               lambda b, i: (b, jnp.maximum(i * (tl // CONV_HALO) - 1, 0), 0))
        zh_spec = pl.BlockSpec((None, POOL_HALO, D_POOL),
                               lambda b, i: (b, jnp.maximum(i * (tl // POOL_HALO) - 1, 0), 0))
        n_pool_prev = 0
    else:
        assert n_tiles == 1
        uh_arr = jnp.pad(conv_hist, ((0, 0), (CONV_HALO - conv_hist.shape[1], 0), (0, 0)))
        zh_arr = jnp.pad(pool_hist, ((0, 0), (POOL_HALO - pool_hist.shape[1], 0), (0, 0)))
        uh_spec = pl.BlockSpec((None, CONV_HALO, D_CONV), lambda b, i: (b, 0, 0))
        zh_spec = pl.BlockSpec((None, POOL_HALO, D_POOL), lambda b, i: (b, 0, 0))
        n_pool_prev = POOL_STATE
    const = lambda shape: _resident(shape, lambda b, i: (0,) * len(shape))
    return pl.pallas_call(
        functools.partial(_mix_kernel, prompt=prompt, n_pool_prev=n_pool_prev),
        grid=(n_seq, n_tiles),
        in_specs=[seq(d_model), seq(D_CONV), uh_spec, seq(D_POOL), zh_spec, seq(D_ATTN), seq(N_BRANCHES * d_model),
                  const((CONV_WIDTH, D_CONV)), const((1, D_CONV)), const((1, D_CONV)), const((1, D_CONV)),
                  const((D_CONV, d_model)), const((len(POOL_WINDOWS), POOL_GROUP, POOL_GROUP)),
                  const((1, D_POOL)), const((D_POOL, d_model)), const((D_ATTN, d_model)),
                  const((d_model, d_model))],
        out_specs=seq(d_model),
        out_shape=jax.ShapeDtypeStruct(x.shape, F32),
        scratch_shapes=[pltpu.VMEM((CONV_HALO + tl + SUBLANES, D_CONV), F32),
                        pltpu.VMEM((POOL_HALO + tl, D_POOL), F32)],
        compiler_params=_cparams(("parallel", "arbitrary")),
    )(x, u, uh_arr, z, zh_arr, attn, gates,
      lw["conv_w"], lw["conv_b"], lw["conv_ln_g"], lw["conv_ln_b"], lw["w_conv_out"], lw["pool_lin"],
      lw["pool_scale"], lw["w_pool_out"], lw["w_attn_out"], lw["w_out"])


def _ffn_chunk(d_ff):
    for n in (2, 4, 1):
        if d_ff % (n * LANES) == 0 and d_ff // n <= 2048:
            return d_ff // n
    return d_ff


def _ffn_kernel(x_ref, gain_ref, w1_ref, w3_ref, w2_ref, fg_ref, o_ref, *, chunk, final_norm):
    x = x_ref[...]
    h = _rms(x, gain_ref[...]).astype(BF16)
    acc = x
    for c0 in range(0, w1_ref.shape[1], chunk):
        act = jax.nn.silu(_dot(h, w1_ref[:, c0:c0 + chunk])) * _dot(h, w3_ref[:, c0:c0 + chunk])
        acc = acc + _dot(act.astype(BF16), w2_ref[c0:c0 + chunk, :])
    o_ref[...] = _rms(acc, fg_ref[...]) if final_norm else acc


def _ffn(x2d, gain, w1, w3, w2, final_gain, *, tm):
    t, d_model = x2d.shape
    d_ff = w1.shape[1]
    final_norm = final_gain is not None
    fg = (final_gain if final_norm else gain).reshape(1, d_model)
    row = pl.BlockSpec((tm, d_model), lambda i: (i, 0))
    vec = _resident((1, d_model), lambda i: (0, 0))
    return pl.pallas_call(
        functools.partial(_ffn_kernel, chunk=_ffn_chunk(d_ff), final_norm=final_norm),
        grid=(t // tm,),
        in_specs=[row, vec, _resident((d_model, d_ff), lambda i: (0, 0)), _resident((d_model, d_ff), lambda i: (0, 0)),
                  _resident((d_ff, d_model), lambda i: (0, 0)), vec],
        out_specs=row,
        out_shape=jax.ShapeDtypeStruct((t, d_model), F32),
        compiler_params=_cparams(("parallel",)),
    )(x2d, gain.reshape(1, d_model), w1, w3, w2, fg)


MOE_CHUNK = 256
MOE_TAIL = LANES


def _dot_halves(a_ref, b):
    half = a_ref.shape[0] // 2
    return jnp.concatenate([_dot(a_ref[:half, :], b), _dot(a_ref[half:, :], b)], axis=0)


def _router_kernel(x_ref, gain_ref, r_ref, tri_ref, rank_ref, rankt_ref, combt_ref, cnt_ref):
    tm = x_ref.shape[0]
    lane = lax.broadcasted_iota(jnp.int32, (tm, LANES), 1)
    h32 = _rms(x_ref[...], gain_ref[...])
    logits = jnp.dot(h32, r_ref[...], preferred_element_type=F32, precision=lax.Precision.HIGHEST)
    logits = jnp.where(lane < N_EXPERTS, logits, -jnp.inf)
    m1 = jnp.max(logits, axis=1, keepdims=True)
    i1 = jnp.min(jnp.where(logits == m1, lane, LANES), axis=1, keepdims=True)
    rest = jnp.where(lane == i1, -jnp.inf, logits)
    m2 = jnp.max(rest, axis=1, keepdims=True)
    i2 = jnp.min(jnp.where(rest == m2, lane, LANES), axis=1, keepdims=True)
    e2 = jnp.exp(m2 - m1)
    den = 1.0 + e2
    comb = jnp.where(lane == i1, 1.0 / den, 0.0) + jnp.where(lane == i2, e2 / den, 0.0)
    routed = (lane == i1) | (lane == i2)
    ones = jnp.where(routed, 1.0, 0.0)
    before = _dot(tri_ref[...], ones.astype(BF16))
    rank = jnp.where(routed, before, -1.0)
    rank_ref[...] = rank
    rankt_ref[...] = rank.T[:N_EXPERTS]
    combt_ref[...] = comb.T[:N_EXPERTS]
    cnt_ref[...] = jnp.sum(ones, axis=0, keepdims=True).astype(jnp.int32)


def _moe_kernel(cnt_ref, x_ref, gain_ref, rank_ref, rankt_ref, combt_ref, w1t_ref, w3t_ref, w2t_ref, fg_ref,
                *refs, final_norm, n_carried):
    o_ref, ht_scr, xt_scr, acct_scr, yt_scr = refs[n_carried:]
    i, e, f = pl.program_id(0), pl.program_id(1), pl.program_id(2)
    n_e, n_f = pl.num_programs(1), pl.num_programs(2)
    tm = x_ref.shape[0]
    n_tail = (jnp.maximum(cnt_ref[i * LANES + e] - MOE_CHUNK, 0) + MOE_TAIL - 1) // MOE_TAIL

    @pl.when((e == 0) & (f == 0))
    def _():
        ht_scr[...] = _rms(x_ref[...], gain_ref[...]).T.astype(BF16)
        yt_scr[...] = jnp.zeros_like(yt_scr)

    def expert_pass(base, width):
        cols = pl.ds(base if isinstance(base, int) else pl.multiple_of(base, MOE_TAIL), width)

        @pl.when(f == 0)
        def _():
            lane = lax.broadcasted_iota(jnp.int32, (tm, LANES), 1)
            rank_col = jnp.sum(jnp.where(lane == e, rank_ref[...], 0.0), axis=1, keepdims=True)
            slot = lax.broadcasted_iota(jnp.int32, (tm, width), 1) + base
            take = jnp.where(rank_col == slot.astype(F32), 1.0, 0.0).astype(BF16)
            xt_scr[:, cols] = _dot_halves(ht_scr, take).astype(BF16)

        xt = xt_scr[:, cols]
        act = jax.nn.silu(_dot(w1t_ref[...], xt)) * _dot(w3t_ref[...], xt)
        part = _dot_halves(w2t_ref, act.astype(BF16))

        @pl.when(f == 0)
        def _():
            acct_scr[:, cols] = part

        @pl.when(f > 0)
        def _():
            acct_scr[:, cols] += part

    expert_pass(0, MOE_CHUNK)

    def tail_pass(c, carry):
        expert_pass(MOE_CHUNK + c * MOE_TAIL, MOE_TAIL)
        return carry

    lax.fori_loop(0, n_tail, tail_pass, 0)

    @pl.when(f == n_f - 1)
    def _():
        rank_row = rankt_ref[pl.ds(e, 1), :]
        comb_row = combt_ref[pl.ds(e, 1), :]

        def scatter_pass(base, width):
            cols = pl.ds(base if isinstance(base, int) else pl.multiple_of(base, MOE_TAIL), width)
            slot = lax.broadcasted_iota(jnp.int32, (width, tm), 0) + base
            put = jnp.where(rank_row == slot.astype(F32), 1.0, 0.0).astype(BF16)
            a = acct_scr[:, cols]
            hi = a.astype(BF16)
            lo = (a - hi.astype(F32)).astype(BF16)
            yt_scr[...] += comb_row * (_dot(hi, put) + _dot(lo, put))

        scatter_pass(0, MOE_CHUNK)

        def tail_scatter(c, carry):
            scatter_pass(MOE_CHUNK + c * MOE_TAIL, MOE_TAIL)
            return carry

        lax.fori_loop(0, n_tail, tail_scatter, 0)

    @pl.when((e == n_e - 1) & (f == n_f - 1))
    def _():
        out = x_ref[...] + yt_scr[...].T
        o_ref[...] = _rms(out, fg_ref[...]) if final_norm else out


def _moe_tiling(t):
    main, rest = 7 * LANES, MOE_CHUNK
    if t > rest and (t - rest) % main == 0:
        return [(main, 0, (t - rest) // main), (rest, (t - rest) // rest, 1)]
    tm = _pick_tile(t, (1024, 512, MOE_CHUNK))
    return [(tm, 0, t // tm)]


def _moe(x2d, gain, router, w1t, w3t, w2t, final_gain, *, tm, tile0, n_tiles, out_carried=None):
    t, d_model = x2d.shape
    n_e, d_ff, _ = w1t.shape
    assert n_e == N_EXPERTS and tm % LANES == 0 and tm >= MOE_CHUNK
    chunk = _ffn_chunk(d_ff)
    n_f = d_ff // chunk
    final_norm = final_gain is not None
    fg = (final_gain if final_norm else gain).reshape(1, d_model)
    gain2 = gain.reshape(1, d_model)
    r_pad = jnp.pad(router, ((0, 0), (0, LANES - n_e)))
    earlier = (jnp.arange(tm)[None, :] < jnp.arange(tm)[:, None]).astype(BF16)

    rank, rank_t, comb_t, counts = pl.pallas_call(
        _router_kernel,
        grid=(n_tiles,),
        in_specs=[pl.BlockSpec((tm, d_model), lambda i: (i + tile0, 0)), _resident((1, d_model), lambda i: (0, 0)),
                  _resident((d_model, LANES), lambda i: (0, 0)), _resident((tm, tm), lambda i: (0, 0))],
        out_specs=[pl.BlockSpec((tm, LANES), lambda i: (i, 0)),
                   pl.BlockSpec((None, N_EXPERTS, tm), lambda i: (i, 0, 0)),
                   pl.BlockSpec((None, N_EXPERTS, tm), lambda i: (i, 0, 0)),
                   pl.BlockSpec((None, 1, LANES), lambda i: (i, 0, 0))],
        out_shape=[jax.ShapeDtypeStruct((n_tiles * tm, LANES), F32),
                   jax.ShapeDtypeStruct((n_tiles, N_EXPERTS, tm), F32),
                   jax.ShapeDtypeStruct((n_tiles, N_EXPERTS, tm), F32),
                   jax.ShapeDtypeStruct((n_tiles, 1, LANES), jnp.int32)],
        compiler_params=_cparams(("parallel",)),
    )(x2d, gain2, r_pad, earlier)

    row = pl.BlockSpec((tm, d_model), lambda i, e, f, cnt: (i + tile0, 0))
    vec = _resident((1, d_model), lambda i, e, f, cnt: (0, 0))
    tile_t = pl.BlockSpec((None, N_EXPERTS, tm), lambda i, e, f, cnt: (i, 0, 0))
    carried = () if out_carried is None else (out_carried,)
    grid_spec = pltpu.PrefetchScalarGridSpec(
        num_scalar_prefetch=1,
        grid=(n_tiles, n_e, n_f),
        in_specs=[row, vec, pl.BlockSpec((tm, LANES), lambda i, e, f, cnt: (i, 0)), tile_t, tile_t,
                  pl.BlockSpec((None, chunk, d_model), lambda i, e, f, cnt: (e, f, 0)),
                  pl.BlockSpec((None, chunk, d_model), lambda i, e, f, cnt: (e, f, 0)),
                  pl.BlockSpec((None, d_model, chunk), lambda i, e, f, cnt: (e, 0, f)), vec]
                 + [pl.BlockSpec(memory_space=pl.ANY)] * len(carried),
        out_specs=row,
        scratch_shapes=[pltpu.VMEM((d_model, tm), BF16), pltpu.VMEM((d_model, tm), BF16),
                        pltpu.VMEM((d_model, tm), F32), pltpu.VMEM((d_model, tm), F32)],
    )
    operands = (counts.reshape(n_tiles * LANES), x2d, gain2, rank, rank_t, comb_t, w1t, w3t, w2t, fg) + carried
    return pl.pallas_call(
        functools.partial(_moe_kernel, final_norm=final_norm, n_carried=len(carried)),
        grid_spec=grid_spec,
        out_shape=jax.ShapeDtypeStruct((t, d_model), F32),
        input_output_aliases={len(operands) - 1: 0} if carried else {},
        compiler_params=_cparams(("parallel", "arbitrary", "arbitrary")),
    )(*operands)


def _pick_tile(n, prefs):
    for p in prefs:
        if n % p == 0:
            return p
    return n


def _forward(x, pos, cache, layers, norm_final, *, prompt):
    n_seq, seq_len, d_model = x.shape
    t = n_seq * seq_len
    depth = len(layers)
    if prompt:
        assert seq_len % MOBA_BLOCK == 0 and seq_len // MOBA_BLOCK <= HEAD_DIM
        tm_in = _pick_tile(seq_len, (512, MOBA_BLOCK))
        tables = _rope_tables(pos)
        tl = _pick_tile(seq_len, (512, 256, 128, 64, 32))
        tm_ffn = _pick_tile(t, (512, 256))
        tm_moe = _pick_tile(t, (1024, 512, 256))
    else:
        tm_in = t
        tables = tuple(jnp.tile(tb, (n_seq, 1)) for tb in _rope_tables(pos))
        tl = seq_len
        tm_ffn = tm_moe = t
        cache_kt, cache_vt, page_table, state_conv, state_pool = cache
    ks, vs, cs, ps = [], [], [], []
    kv_t = None
    for l, lw in enumerate(layers):
        outs = _inproj(x.reshape(t, d_model), lw["norm_mix"], lw["w_in_aug"], tables,
                       prompt=prompt, seq_len=seq_len, tm=tm_in, layer=l, depth=depth, kv_carried=kv_t)
        if prompt:
            q_aug, k_aug, kt_all, vt_all, v_bf, kmean, u, z, gates = outs
            kv_t = (kt_all, vt_all)
            attn = _attn_prompt(q_aug, k_aug, v_bf, kmean, n_seq=n_seq, seq_len=seq_len, tq=2 * MOBA_BLOCK)
            conv_hist = pool_hist = None
        else:
            q_aug, k_new, v_new, u, z, gates = outs
            attn = _attn_sample(q_aug, k_new, v_new, cache_kt, cache_vt, page_table, l,
                                n_seq=n_seq, seq_len=seq_len)
            conv_hist, pool_hist = state_conv[l], state_pool[l]
        r3 = lambda a: a.reshape(n_seq, seq_len, a.shape[-1])
        u3, z3 = r3(u), r3(z)
        x = _mix(x, u3, z3, r3(attn), r3(gates), conv_hist, pool_hist, lw, prompt=prompt, tl=tl)
        final_gain = norm_final if l == depth - 1 else None
        x2d = x.reshape(t, d_model)
        if lw["kind"] == "dense":
            x2d = _ffn(x2d, lw["norm_ffn"], lw["w1"], lw["w3"], lw["w2"], final_gain, tm=tm_ffn)
        else:
            out = None
            for tm_moe, tile0, n_tiles in _moe_tiling(t):
                out = _moe(x2d, lw["norm_ffn"], lw["router"], lw["w1"], lw["w3"], lw["w2"], final_gain,
                           tm=tm_moe, tile0=tile0, n_tiles=n_tiles, out_carried=out)
            x2d = out
        x = x2d.reshape(n_seq, seq_len, d_model)
        if prompt:
            cs.append(u3[:, seq_len - (CONV_WIDTH - 1):])
            ps.append(z3[:, seq_len - POOL_STATE:])
        else:
            ks.append(k_new.reshape(n_seq, seq_len, N_HEADS, HEAD_DIM))
            vs.append(v_new.reshape(n_seq, seq_len, N_HEADS, HEAD_DIM))
            cs.append(jnp.concatenate([conv_hist, u3], axis=1)[:, -(CONV_WIDTH - 1):])
            ps.append(jnp.concatenate([pool_hist, z3], axis=1)[:, -POOL_STATE:])
    if prompt:
        untranspose = lambda a: a.reshape(depth, n_seq, N_HEADS, HEAD_DIM, seq_len).transpose(0, 1, 4, 2, 3)
        k_out, v_out = untranspose(kv_t[0]), untranspose(kv_t[1])
    else:
        k_out, v_out = jnp.stack(ks), jnp.stack(vs)
    return x, k_out, v_out, jnp.stack(cs), jnp.stack(ps)


def kernel(x_prompt, x_sample, cache_k, cache_v, page_table, state_conv, state_pool, norm_mix, norm_ffn, norm_final, w_in, w_attn_out, conv_w, conv_b, conv_ln_g, conv_ln_b, w_conv_out, pool_lin, pool_scale, w_pool_out, w_out, ffn_w1, ffn_w3, ffn_w2, moe_router, moe_w1, moe_w3, moe_w2):
    depth = norm_mix.shape[0]
    layers = []
    for l in range(depth):
        lw = dict(
            norm_mix=norm_mix[l], norm_ffn=norm_ffn[l],
            w_in_aug=_prep_w_in(w_in[l]),
            w_attn_out=w_attn_out[l].astype(BF16), conv_w=conv_w[l], conv_b=conv_b[l].reshape(1, -1),
            conv_ln_g=conv_ln_g[l].reshape(1, -1), conv_ln_b=conv_ln_b[l].reshape(1, -1),
            w_conv_out=w_conv_out[l].astype(BF16), pool_lin=pool_lin[l].astype(BF16),
            pool_scale=pool_scale[l].reshape(1, -1), w_pool_out=w_pool_out[l].astype(BF16),
            w_out=w_out[l].astype(BF16))
        i = l // 2
        if l % 2 == 0:
            lw.update(kind="dense", w1=ffn_w1[i].astype(BF16), w3=ffn_w3[i].astype(BF16), w2=ffn_w2[i].astype(BF16))
        else:
            tr = lambda w: jnp.swapaxes(w, 1, 2).astype(BF16)
            lw.update(kind="moe", router=moe_router[i], w1=tr(moe_w1[i]), w3=tr(moe_w3[i]), w2=tr(moe_w2[i]))
        layers.append(lw)

    past_len = page_table.shape[1] * cache_k.shape[2]
    pos_prompt = jnp.arange(x_prompt.shape[1], dtype=jnp.int32)
    pos_sample = past_len + jnp.arange(x_sample.shape[1], dtype=jnp.int32)
    y_p, k_p, v_p, conv_p, pool_p = _forward(x_prompt, pos_prompt, None, layers, norm_final, prompt=True)
    cache = (_transposed_pages(cache_k), _transposed_pages(cache_v), page_table, state_conv, state_pool)
    y_s, k_s, v_s, conv_s, pool_s = _forward(x_sample, pos_sample, cache, layers, norm_final, prompt=False)
    return (y_p, y_s, k_p, v_p, k_s, v_s, conv_p, conv_s, pool_p, pool_s)
```
